```python
import jax
import jax.numpy as jnp
from jax import lax
import numpy as np

D_MODEL = 2048
BATCH = 2
SEQ = 16384
DEPTH = 1
DEC_BATCH = 32
DEC_SEQ = 64
PAST_LEN = 2048

CHUNK = 64
BLOCK = 16
HG_EXPAND = 128
HG_WIDTH = D_MODEL // 2
HG_HEADS = HG_WIDTH // HG_EXPAND
HG_DK = HG_EXPAND
HG_DV = HG_WIDTH // HG_HEADS
GLA_WIDTH = D_MODEL - HG_WIDTH
GLA_HEADS = 4
GLA_DV = GLA_WIDTH // GLA_HEADS
GLA_DK = GLA_DV // 2
GLA_RANK = 16
GLA_GATE_NORMALIZER = 16.0
D_FF = ((8 * D_MODEL + 3 * 256 - 1) // (3 * 256)) * 256
EPS = 1e-6
IN_SIZES = (HG_HEADS * HG_DK, HG_HEADS * HG_DK, HG_HEADS * HG_DV, HG_WIDTH,
            GLA_HEADS * GLA_DK, GLA_HEADS * GLA_DK, GLA_HEADS * GLA_DV, GLA_WIDTH, GLA_RANK)
N_IN = sum(IN_SIZES)

kernel_name = "hybrid_hgrn2_gla_streaming_step"


def rmsnorm(x, w):
    xf = x.astype(jnp.float32)
    y = xf * lax.rsqrt(jnp.mean(xf * xf, axis=-1, keepdims=True) + EPS)
    return (y * w.astype(jnp.float32)).astype(x.dtype)


def split_cols(p, sizes):
    outs, off = [], 0
    for s in sizes:
        outs.append(p[..., off:off + s])
        off += s
    return outs


def to_heads(t, n_heads):
    b, t_len, _ = t.shape
    return t.reshape(b, t_len, n_heads, -1).transpose(0, 2, 1, 3)


def head_norm(o, w, dtype):
    y = o * lax.rsqrt(jnp.mean(o * o, axis=-1, keepdims=True) + EPS) * w.astype(jnp.float32)
    b, h, t_len, d = y.shape
    return y.transpose(0, 2, 1, 3).reshape(b, t_len, h * d).astype(dtype)


def gated_linear_recurrence(q, k, v, log_a, s0):
    b_sz, h, t_len, _ = q.shape
    dv = v.shape[-1]
    pad = (-t_len) % BLOCK
    nb = (t_len + pad) // BLOCK

    def blocks(t):
        t = jnp.pad(t.astype(jnp.float32), ((0, 0), (0, 0), (0, pad), (0, 0)))
        return jnp.moveaxis(t.reshape(b_sz, h, nb, BLOCK, t.shape[-1]), 2, 0)

    mask = jnp.tril(jnp.ones((BLOCK, BLOCK), dtype=bool))[:, :, None]

    def step(S, inp):
        qb, kb, vb, ab = inp
        cum = jnp.cumsum(ab, axis=2)
        o_inter = jnp.einsum('bhtk,bhkv->bhtv', qb * jnp.exp(cum), S)
        diff = cum[:, :, :, None, :] - cum[:, :, None, :, :]
        decay = jnp.exp(jnp.where(mask, diff, -jnp.inf))
        scores = jnp.einsum('bhtk,bhtsk,bhsk->bhts', qb, decay, kb)
        o = o_inter + jnp.einsum('bhts,bhsv->bhtv', scores, vb)
        last = cum[:, :, -1:, :]
        S = (jnp.exp(last[:, :, 0, :])[..., None] * S
             + jnp.einsum('bhsk,bhsv->bhkv', kb * jnp.exp(last - cum), vb))
        return S, o

    s_fin, o = lax.scan(step, s0.astype(jnp.float32), (blocks(q), blocks(k), blocks(v), blocks(log_a)))
    o = jnp.moveaxis(o, 0, 2).reshape(b_sz, h, nb * BLOCK, dv)[:, :, :t_len]
    return o, s_fin.astype(s0.dtype)


def mixer(h, s_hg, s_gla, lb, w_in, w_gk2, b_gk, hg_norm, gla_norm, w_out):
    p = h @ w_in
    hq, hf, hi, hgate, gq, gk, gv, ggate, gr = split_cols(p, IN_SIZES)
    zf = hf.astype(jnp.float32)
    lbf = lb.astype(jnp.float32)
    log_f = jnp.logaddexp(jnp.log(lbf), jnp.log1p(-lbf) + jax.nn.log_sigmoid(zf))
    k_hg = (1.0 - lbf) * jax.nn.sigmoid(-zf)
    q_hg = jax.nn.silu(hq.astype(jnp.float32)) * (HG_DK ** -0.5)
    o_hg, s_hg_new = gated_linear_recurrence(to_heads(q_hg, HG_HEADS), to_heads(k_hg, HG_HEADS),
                                             to_heads(hi, HG_HEADS), to_heads(log_f, HG_HEADS), s_hg)
    log_a = jax.nn.log_sigmoid((gr @ w_gk2 + b_gk).astype(jnp.float32)) / GLA_GATE_NORMALIZER
    q_g = gq.astype(jnp.float32) * (GLA_DK ** -0.5)
    o_gla, s_gla_new = gated_linear_recurrence(to_heads(q_g, GLA_HEADS), to_heads(gk, GLA_HEADS),
                                               to_heads(gv, GLA_HEADS), to_heads(log_a, GLA_HEADS), s_gla)
    o = jnp.concatenate([head_norm(o_hg, hg_norm, h.dtype) * jax.nn.silu(hgate),
                         head_norm(o_gla, gla_norm, h.dtype) * jax.nn.silu(ggate)], axis=-1)
    return o @ w_out, s_hg_new, s_gla_new


def swiglu(h, w_gate_up, w_down):
    g, u = jnp.split(h @ w_gate_up, 2, axis=-1)
    return (jax.nn.silu(g) * u) @ w_down


def trunk(x, s_hg, s_gla, lb_logits, norm_mix, w_in, w_gk2, b_gk, hg_norm, gla_norm, w_out,
          norm_ffn, w_gate_up, w_down, norm_final):
    lb_all = jnp.cumsum(jax.nn.softmax(lb_logits.astype(jnp.float32), axis=0), axis=0)[:DEPTH]
    new_hg, new_gla = [], []
    for l in range(DEPTH):
        m, a, b = mixer(rmsnorm(x, norm_mix[l]), s_hg[l], s_gla[l], lb_all[l], w_in[l], w_gk2[l], b_gk[l],
                        hg_norm[l], gla_norm[l], w_out[l])
        x = x + m
        x = x + swiglu(rmsnorm(x, norm_ffn[l]), w_gate_up[l], w_down[l])
        new_hg.append(a)
        new_gla.append(b)
    return rmsnorm(x, norm_final), jnp.stack(new_hg), jnp.stack(new_gla)


def setup_inputs(seed: int = 0) -> dict:
    key = jax.random.key(seed)
    ks = jax.random.split(key, 20)
    f32 = jnp.float32
    nrm = lambda k, shape, s: jax.random.normal(k, shape, f32) * s
    return {
        'x_prompt': nrm(ks[0], (BATCH, SEQ, D_MODEL), 1.0),
        'x_sample': nrm(ks[1], (DEC_BATCH, DEC_SEQ, D_MODEL), 1.0),
        'state_hgrn': nrm(ks[2], (DEPTH, DEC_BATCH, HG_HEADS, HG_DK, HG_DV), 0.5),
        'state_gla': nrm(ks[3], (DEPTH, DEC_BATCH, GLA_HEADS, GLA_DK, GLA_DV), 1.0),
        'lb_logits': nrm(ks[4], (DEPTH + 1, HG_HEADS * HG_DK), 0.1),
        'norm_mix': 1.0 + nrm(ks[5], (DEPTH, D_MODEL), 0.02),
        'w_in': nrm(ks[6], (DEPTH, D_MODEL, N_IN), D_MODEL ** -0.5),
        'w_gk2': nrm(ks[7], (DEPTH, GLA_RANK, GLA_HEADS * GLA_DK), GLA_RANK ** -0.5),
        'b_gk': nrm(ks[8], (DEPTH, GLA_HEADS * GLA_DK), 0.02),
        'hg_norm': 1.0 + nrm(ks[9], (DEPTH, HG_DV), 0.02),
        'gla_norm': 1.0 + nrm(ks[10], (DEPTH, GLA_DV), 0.02),
        'w_out': nrm(ks[11], (DEPTH, D_MODEL, D_MODEL), D_MODEL ** -0.5),
        'norm_ffn': 1.0 + nrm(ks[12], (DEPTH, D_MODEL), 0.02),
        'w_gate_up': nrm(ks[13], (DEPTH, D_MODEL, 2 * D_FF), D_MODEL ** -0.5),
        'w_down': nrm(ks[14], (DEPTH, D_FF, D_MODEL), D_FF ** -0.5),
        'norm_final': 1.0 + nrm(ks[15], (D_MODEL,), 0.02),
    }


def reference(x_prompt, x_sample, state_hgrn, state_gla, lb_logits, norm_mix, w_in, w_gk2, b_gk,
              hg_norm, gla_norm, w_out, norm_ffn, w_gate_up, w_down, norm_final):
    b_p = x_prompt.shape[0]
    zero_hg = jnp.zeros((DEPTH, b_p, HG_HEADS, HG_DK, HG_DV), x_prompt.dtype)
    zero_gla = jnp.zeros((DEPTH, b_p, GLA_HEADS, GLA_DK, GLA_DV), x_prompt.dtype)
    y_prompt, st_hg_p, st_gla_p = trunk(x_prompt, zero_hg, zero_gla, lb_logits, norm_mix, w_in, w_gk2, b_gk,
                                        hg_norm, gla_norm, w_out, norm_ffn, w_gate_up, w_down, norm_final)
    y_sample, st_hg_s, st_gla_s = trunk(x_sample, state_hgrn, state_gla, lb_logits, norm_mix, w_in, w_gk2, b_gk,
                                        hg_norm, gla_norm, w_out, norm_ffn, w_gate_up, w_down, norm_final)
    return (y_prompt, y_sample, st_hg_p, st_gla_p, st_hg_s, st_gla_s)
```

```python
import functools

import numpy as np
import jax
import jax.numpy as jnp
from jax import lax
from jax.experimental import pallas as pl
from jax.experimental.pallas import tpu as pltpu

F32 = jnp.float32
BF16 = jnp.bfloat16

D_MODEL = 2048
HG_HEADS = 8
HG_DK = 128
HG_DV = 128
GLA_HEADS = 4
GLA_DK = 128
GLA_DV = 256
GLA_RANK = 16
GLA_GATE_NORMALIZER = 16.0
D_FF = 5632
EPS = 1e-6

LANES = 128
CHUNK = 128
LEVELS = 7
VMEM_LIMIT = 56 * 1024 * 1024

PA_COLS = 6144
PB_COLS = 1152


def _rms(x, w):
    ms = jnp.mean(x * x, axis=-1, keepdims=True)
    return (x * lax.rsqrt(ms + EPS)) * w


def _sigmoid(x):
    return 1.0 / (1.0 + jnp.exp(-x))


def _inproj_kernel(x_ref, nw_ref, w_ref, o_ref, h_scr):
    @pl.when(pl.program_id(1) == 0)
    def _():
        h_scr[...] = _rms(x_ref[...], nw_ref[...]).astype(BF16)

    o_ref[...] = jnp.dot(h_scr[...], w_ref[...], preferred_element_type=F32).astype(o_ref.dtype)


def _inproj(x, norm_w, w, out_dtype, tm, tn):
    t, d = x.shape
    n = w.shape[1]
    return pl.pallas_call(
        _inproj_kernel,
        grid=(t // tm, n // tn),
        in_specs=[
            pl.BlockSpec((tm, d), lambda i, j: (i, 0)),
            pl.BlockSpec((1, d), lambda i, j: (0, 0)),
            pl.BlockSpec((d, tn), lambda i, j: (0, j)),
        ],
        out_specs=pl.BlockSpec((tm, tn), lambda i, j: (i, j)),
        out_shape=jax.ShapeDtypeStruct((t, n), out_dtype),
        scratch_shapes=[pltpu.VMEM((tm, d), BF16)],
        compiler_params=pltpu.CompilerParams(
            dimension_semantics=("parallel", "arbitrary"), vmem_limit_bytes=VMEM_LIMIT),
        name="inproj",
    )(x, norm_w, w)


def _chunk_constants():
    c = CHUNK
    t = np.arange(c)[:, None]
    u = np.arange(c)[None, :]
    mats = []
    for l in range(LEVELS):
        h = 1 << l
        mid = (t // (2 * h)) * (2 * h) + h - 1
        upper = (t & h) != 0
        m = np.where(upper, (u > mid) & (u <= t), (u > t) & (u <= mid))
        mats.append(m)
    mats.append(u <= t)
    mats.append(u > t)
    mall = np.concatenate(mats, axis=0).astype(np.float32)
    x = t ^ u
    lev = np.where(t > u, np.floor(np.log2(np.maximum(x, 1))).astype(np.int32), -1)
    lev = np.where(t == u, LEVELS, lev).astype(np.int32)
    return mall, lev


def _chunk_step(q, k, a, v, s, mall, lev):
    c = CHUNK
    a_hi = a.astype(BF16)
    rem = a - a_hi.astype(F32)
    a_mid = rem.astype(BF16)
    a_lo = (rem - a_mid.astype(F32)).astype(BF16)
    e3 = jnp.dot(mall, jnp.concatenate([a_hi, a_mid, a_lo], axis=1), preferred_element_type=F32)
    e = e3[:, :LANES] + e3[:, LANES:2 * LANES] + e3[:, 2 * LANES:]

    row = lax.broadcasted_iota(jnp.int32, (c, LANES), 0)
    diag = jnp.sum(q * k, axis=1, keepdims=True)
    scores = jnp.where(lev == LEVELS, diag, 0.0)
    for l in range(LEVELS):
        upper = (row & (1 << l)) != 0
        x = (jnp.where(upper, q, k) * jnp.exp(e[l * c:(l + 1) * c])).astype(BF16)
        sc = lax.dot_general(x, x, (((1,), (1,)), ((), ())), preferred_element_type=F32)
        scores = jnp.where(lev == l, sc, scores)

    cum = e[LEVELS * c:(LEVELS + 1) * c]
    rev = e[(LEVELS + 1) * c:(LEVELS + 2) * c]
    qc = (q * jnp.exp(cum)).astype(BF16)
    kr = (k * jnp.exp(rev)).astype(BF16)
    o = (jnp.dot(qc, s.astype(BF16), preferred_element_type=F32)
         + jnp.dot(scores.astype(BF16), v, preferred_element_type=F32))
    dcol = jnp.exp(jnp.broadcast_to(cum[c - 1:c, :], (LANES, LANES))).T
    if s.shape[1] > LANES:
        dcol = jnp.concatenate([dcol] * (s.shape[1] // LANES), axis=1)
    s_new = dcol * s + lax.dot_general(kr, v, (((0,), (0,)), ((), ())), preferred_element_type=F32)
    return o, s_new


def _pad_rows(x, rows):
    if x.shape[0] == rows:
        return x
    return jnp.concatenate([x, jnp.zeros((rows - x.shape[0], x.shape[1]), x.dtype)], axis=0)


def _head_out(o, norm_w, gate):
    ms = jnp.mean(o * o, axis=-1, keepdims=True)
    y = o * lax.rsqrt(ms + EPS) * norm_w
    g = gate.astype(F32)
    return (y * (g * _sigmoid(g))).astype(BF16)


def _recurrence_loop(load_chunk, store_chunk, s_scr, mall_ref, lev_ref, bb, tb):
    rows = min(tb, CHUNK)
    nch = tb // rows

    def stream(b, carry):
        def chunk(ci, carry2):
            r0 = pl.multiple_of(ci * rows, rows)
            q, k, a, v, gate = load_chunk(b, r0, rows)
            q, k, a, v = (_pad_rows(z, CHUNK) for z in (q, k, a, v))
            o, s_new = _chunk_step(q, k, a, v, s_scr[b], mall_ref[...], lev_ref[...])
            s_scr[b] = s_new
            store_chunk(b, r0, rows, o[:rows], gate)
            return carry2

        return lax.fori_loop(0, nch, chunk, carry)

    lax.fori_loop(0, bb, stream, 0)


def _hgrn_kernel(hq_ref, hf_ref, hi_ref, hg_ref, s0_ref, lbl_ref, nw_ref, mall_ref, lev_ref,
                 o_ref, sout_ref, s_scr, *, bb, tb):
    ti = pl.program_id(2)

    @pl.when(ti == 0)
    def _():
        s_scr[...] = s0_ref[...]

    l0 = lbl_ref[0:1, :]
    l1 = lbl_ref[1:2, :]
    mx = jnp.maximum(l0, l1)
    e0 = jnp.exp(l0 - mx)
    e1 = jnp.exp(l1 - mx)
    lb = e0 / (e0 + e1)

    def load_chunk(b, r0, rows):
        sl = pl.ds(r0, rows)
        zq = hq_ref[b, sl, :].astype(F32)
        q = zq * _sigmoid(zq) * (HG_DK ** -0.5)
        z = hf_ref[b, sl, :]
        ez = jnp.exp(-jnp.abs(z))
        big = 1.0 / (1.0 + ez)
        small = ez * big
        pos = z >= 0.0
        k = (1.0 - lb) * jnp.where(pos, small, big)
        a = jnp.log(lb + (1.0 - lb) * jnp.where(pos, big, small))
        return q, k, a, hi_ref[b, sl, :], hg_ref[b, sl, :]

    def store_chunk(b, r0, rows, o, gate):
        o_ref[b, pl.ds(r0, rows), :] = _head_out(o, nw_ref[...], gate)

    _recurrence_loop(load_chunk, store_chunk, s_scr, mall_ref, lev_ref, bb, tb)

    @pl.when(ti == pl.num_programs(2) - 1)
    def _():
        sout_ref[...] = s_scr[...]


def _gla_kernel(gq_ref, gk_ref, gv_ref, gg_ref, gr_ref, w2_ref, b2_ref, s0_ref, nw_ref, mall_ref, lev_ref,
                o_ref, sout_ref, s_scr, *, bb, tb):
    ti = pl.program_id(2)

    @pl.when(ti == 0)
    def _():
        s_scr[...] = s0_ref[...]

    def load_chunk(b, r0, rows):
        sl = pl.ds(r0, rows)
        q = gq_ref[b, sl, :].astype(F32) * (GLA_DK ** -0.5)
        k = gk_ref[b, sl, :].astype(F32)
        x = jnp.dot(gr_ref[b, sl, :].astype(BF16), w2_ref[...], preferred_element_type=F32) + b2_ref[...]
        a = (jnp.minimum(x, 0.0) - jnp.log(1.0 + jnp.exp(-jnp.abs(x)))) * (1.0 / GLA_GATE_NORMALIZER)
        return q, k, a, gv_ref[b, sl, :], gg_ref[b, sl, :]

    def store_chunk(b, r0, rows, o, gate):
        o_ref[b, pl.ds(r0, rows), :] = _head_out(o, nw_ref[...], gate)

    _recurrence_loop(load_chunk, store_chunk, s_scr, mall_ref, lev_ref, bb, tb)

    @pl.when(ti == pl.num_programs(2) - 1)
    def _():
        sout_ref[...] = s_scr[...]


def _rec_blocks(t_len, batch):
    tb = min(t_len, 4 * CHUNK)
    bb = 1 if t_len > CHUNK else min(batch, 8)
    return bb, tb


def _hgrn(pa, pb, s0, lb_logits, norm_w, mall, lev):
    b, t, _ = pa.shape
    bb, tb = _rec_blocks(t, b)
    blk = lambda off: pl.BlockSpec((bb, tb, LANES), lambda bi, h, ti: (bi, ti, off + h))
    const = lambda shape: pl.BlockSpec(shape, lambda bi, h, ti: (0,) * len(shape))
    st = pl.BlockSpec((bb, None, HG_DK, HG_DV), lambda bi, h, ti: (bi, h, 0, 0))
    return pl.pallas_call(
        functools.partial(_hgrn_kernel, bb=bb, tb=tb),
        grid=(b // bb, HG_HEADS, t // tb),
        in_specs=[blk(0), blk(0), blk(8), blk(16), st,
                  pl.BlockSpec((2, LANES), lambda bi, h, ti: (0, h)),
                  const((1, HG_DV)), const(mall.shape), const(lev.shape)],
        out_specs=[blk(0), st],
        out_shape=[jax.ShapeDtypeStruct((b, t, HG_HEADS * HG_DV), BF16),
                   jax.ShapeDtypeStruct(s0.shape, F32)],
        scratch_shapes=[pltpu.VMEM((bb, HG_DK, HG_DV), F32)],
        compiler_params=pltpu.CompilerParams(
            dimension_semantics=("parallel", "parallel", "arbitrary"), vmem_limit_bytes=VMEM_LIMIT),
        name="hgrn",
    )(pa, pb, pa, pa, s0, lb_logits, norm_w, mall, lev)


def _gla(pa, pb, s0, w2, b2, norm_w, mall, lev):
    b, t, _ = pa.shape
    bb, tb = _rec_blocks(t, b)
    blk = lambda width, off: pl.BlockSpec((bb, tb, width), lambda bi, h, ti: (bi, ti, off + h))
    const = lambda shape: pl.BlockSpec(shape, lambda bi, h, ti: (0,) * len(shape))
    st = pl.BlockSpec((bb, None, GLA_DK, GLA_DV), lambda bi, h, ti: (bi, h, 0, 0))
    return pl.pallas_call(
        functools.partial(_gla_kernel, bb=bb, tb=tb),
        grid=(b // bb, GLA_HEADS, t // tb),
        in_specs=[blk(LANES, 24), blk(LANES, 28), blk(GLA_DV, 16), blk(GLA_DV, 20),
                  pl.BlockSpec((bb, tb, LANES), lambda bi, h, ti: (bi, ti, 8)),
                  pl.BlockSpec((LANES, LANES), lambda bi, h, ti: (0, h)),
                  pl.BlockSpec((1, LANES), lambda bi, h, ti: (0, h)),
                  st, const((1, GLA_DV)), const(mall.shape), const(lev.shape)],
        out_specs=[blk(GLA_DV, 0), st],
        out_shape=[jax.ShapeDtypeStruct((b, t, GLA_HEADS * GLA_DV), BF16),
                   jax.ShapeDtypeStruct(s0.shape, F32)],
        scratch_shapes=[pltpu.VMEM((bb, GLA_DK, GLA_DV), F32)],
        compiler_params=pltpu.CompilerParams(
            dimension_semantics=("parallel", "parallel", "arbitrary"), vmem_limit_bytes=VMEM_LIMIT),
        name="gla",
    )(pa, pa, pa, pa, pb, w2, b2, s0, norm_w, mall, lev)


def _outproj_kernel(ohg_ref, ogla_ref, wt_ref, wb_ref, x_ref, nw_ref, x1_ref, h2_ref):
    m = (jnp.dot(ohg_ref[...], wt_ref[...], preferred_element_type=F32)
         + jnp.dot(ogla_ref[...], wb_ref[...], preferred_element_type=F32))
    x1 = x_ref[...] + m
    x1_ref[...] = x1
    h2_ref[...] = _rms(x1, nw_ref[...]).astype(BF16)


def _outproj(o_hg, o_gla, w_top, w_bot, x, norm_w, tm):
    t, d = x.shape
    half = o_hg.shape[1]
    return pl.pallas_call(
        _outproj_kernel,
        grid=(t // tm,),
        in_specs=[
            pl.BlockSpec((tm, half), lambda i: (i, 0)),
            pl.BlockSpec((tm, half), lambda i: (i, 0)),
            pl.BlockSpec((half, d), lambda i: (0, 0)),
            pl.BlockSpec((half, d), lambda i: (0, 0)),
            pl.BlockSpec((tm, d), lambda i: (i, 0)),
            pl.BlockSpec((1, d), lambda i: (0, 0)),
        ],
        out_specs=[pl.BlockSpec((tm, d), lambda i: (i, 0)), pl.BlockSpec((tm, d), lambda i: (i, 0))],
        out_shape=[jax.ShapeDtypeStruct((t, d), F32), jax.ShapeDtypeStruct((t, d), BF16)],
        compiler_params=pltpu.CompilerParams(
            dimension_semantics=("parallel",), vmem_limit_bytes=VMEM_LIMIT),
        name="outproj",
    )(o_hg, o_gla, w_top, w_bot, x, norm_w)


def _ffn_kernel(h_ref, wg_ref, wu_ref, wd_ref, x1_ref, nw_ref, o_ref, acc_scr):
    j = pl.program_id(1)
    h = h_ref[...]
    g = jnp.dot(h, wg_ref[...], preferred_element_type=F32)
    u = jnp.dot(h, wu_ref[...], preferred_element_type=F32)
    mid = ((g * _sigmoid(g)) * u).astype(BF16)
    d = jnp.dot(mid, wd_ref[...], preferred_element_type=F32)

    @pl.when(j == 0)
    def _():
        acc_scr[...] = d

    @pl.when(j > 0)
    def _():
        acc_scr[...] += d

    @pl.when(j == pl.num_programs(1) - 1)
    def _():
        o_ref[...] = _rms(x1_ref[...] + acc_scr[...], nw_ref[...])


def _ffn(h2, w_gate_up, w_down, x1, norm_w, tm, tf):
    t, d = x1.shape
    nf = D_FF // tf
    return pl.pallas_call(
        _ffn_kernel,
        grid=(t // tm, nf),
        in_specs=[
            pl.BlockSpec((tm, d), lambda i, j: (i, 0)),
            pl.BlockSpec((d, tf), lambda i, j: (0, j)),
            pl.BlockSpec((d, tf), lambda i, j: (0, nf + j)),
            pl.BlockSpec((tf, d), lambda i, j: (j, 0)),
            pl.BlockSpec((tm, d), lambda i, j: (i, 0)),
            pl.BlockSpec((1, d), lambda i, j: (0, 0)),
        ],
        out_specs=pl.BlockSpec((tm, d), lambda i, j: (i, 0)),
        out_shape=jax.ShapeDtypeStruct((t, d), F32),
        scratch_shapes=[pltpu.VMEM((tm, d), F32)],
        compiler_params=pltpu.CompilerParams(
            dimension_semantics=("parallel", "arbitrary"), vmem_limit_bytes=VMEM_LIMIT),
        name="ffn",
    )(h2, w_gate_up, w_gate_up, w_down, x1, norm_w)


def _trunk(x, s_hg, s_gla, wts, consts):
    b, t, d = x.shape
    xf = x.reshape(b * t, d)
    mall, lev = consts
    pa = _inproj(xf, wts["norm_mix"], wts["w_a"], BF16, 1024, 1024).reshape(b, t, PA_COLS)
    pb = _inproj(xf, wts["norm_mix"], wts["w_b"], F32, 1024, PB_COLS).reshape(b, t, PB_COLS)
    o_hg, s_hg_new = _hgrn(pa, pb, s_hg, wts["lb_logits"], wts["hg_norm"], mall, lev)
    o_gla, s_gla_new = _gla(pa, pb, s_gla, wts["w_gk2"], wts["b_gk"], wts["gla_norm"], mall, lev)
    x1, h2 = _outproj(o_hg.reshape(b * t, -1), o_gla.reshape(b * t, -1), wts["w_out_top"], wts["w_out_bot"],
                      xf, wts["norm_ffn"], 512)
    y = _ffn(h2, wts["w_gate_up"], wts["w_down"], x1, wts["norm_final"], 512, 512)
    return y.reshape(b, t, d), s_hg_new[None], s_gla_new[None]


def kernel(x_prompt, x_sample, state_hgrn, state_gla, lb_logits, norm_mix, w_in, w_gk2, b_gk, hg_norm, gla_norm,
           w_out, norm_ffn, w_gate_up, w_down, norm_final):
    w = w_in[0]
    hw = HG_HEADS * HG_DK
    w_a = jnp.concatenate([w[:, 0:hw], w[:, 2 * hw:4 * hw], w[:, 4 * hw:4 * hw + 3072]], axis=1).astype(BF16)
    w_b = jnp.concatenate([w[:, hw:2 * hw], w[:, N_GR:N_GR + GLA_RANK],
                           jnp.zeros((D_MODEL, LANES - GLA_RANK), F32)], axis=1).astype(BF16)
    w2 = jnp.concatenate([w_gk2[0], jnp.zeros((LANES - GLA_RANK, GLA_HEADS * GLA_DK), F32)], axis=0).astype(BF16)
    wo = w_out[0].astype(BF16)
    wts = {
        "norm_mix": norm_mix[0][None, :], "w_a": w_a, "w_b": w_b,
        "lb_logits": lb_logits, "hg_norm": hg_norm[0][None, :],
        "w_gk2": w2, "b_gk": b_gk[0][None, :], "gla_norm": gla_norm[0][None, :],
        "w_out_top": wo[:hw], "w_out_bot": wo[hw:], "norm_ffn": norm_ffn[0][None, :],
        "w_gate_up": w_gate_up[0].astype(BF16), "w_down": w_down[0].astype(BF16),
        "norm_final": norm_final[None, :],
    }
    mall_np, lev_np = _chunk_constants()
    consts = (jnp.asarray(mall_np, BF16), jnp.asarray(lev_np))

    bp = x_prompt.shape[0]
    zero_hg = jnp.zeros((bp, HG_HEADS, HG_DK, HG_DV), F32)
    zero_gla = jnp.zeros((bp, GLA_HEADS, GLA_DK, GLA_DV), F32)
    y_p, hg_p, gla_p = _trunk(x_prompt, zero_hg, zero_gla, wts, consts)
    y_s, hg_s, gla_s = _trunk(x_sample, state_hgrn[0], state_gla[0], wts, consts)
    return (y_p, y_s, hg_p, gla_p, hg_s, gla_s)


N_GR = 4 * HG_HEADS * HG_DK + 3072
```

```python
import functools

import numpy as np
import jax
import jax.numpy as jnp
from jax import lax
from jax.experimental import pallas as pl
from jax.experimental.pallas import tpu as pltpu

F32 = jnp.float32
BF16 = jnp.bfloat16

D_MODEL = 2048
HG_HEADS = 8
HG_DK = 128
HG_DV = 128
GLA_HEADS = 4
GLA_DK = 128
GLA_DV = 256
GLA_RANK = 16
GLA_GATE_NORMALIZER = 16.0
D_FF = 5632
EPS = 1e-6
LOG2E = 1.4426950408889634

LANES = 128
CHUNK = 128
LEVELS = 7
MXU_LEVELS = 3
VMEM_LIMIT = 56 * 1024 * 1024

PA_COLS = 6144
PB_COLS = 1152


def _rms(x, w):
    ms = jnp.mean(x * x, axis=-1, keepdims=True)
    return (x * lax.rsqrt(ms + EPS)) * w


def _sigmoid(x):
    return 1.0 / (1.0 + jnp.exp(-x))


def _inproj_kernel(x_ref, nw_ref, w_ref, o_ref, h_scr):
    @pl.when(pl.program_id(1) == 0)
    def _():
        h_scr[...] = _rms(x_ref[...], nw_ref[...]).astype(BF16)

    o_ref[...] = jnp.dot(h_scr[...], w_ref[...], preferred_element_type=F32).astype(o_ref.dtype)


def _inproj(x, norm_w, w, out_dtype, tm, tn):
    t, d = x.shape
    n = w.shape[1]
    return pl.pallas_call(
        _inproj_kernel,
        grid=(t // tm, n // tn),
        in_specs=[
            pl.BlockSpec((tm, d), lambda i, j: (i, 0)),
            pl.BlockSpec((1, d), lambda i, j: (0, 0)),
            pl.BlockSpec((d, tn), lambda i, j: (0, j)),
        ],
        out_specs=pl.BlockSpec((tm, tn), lambda i, j: (i, j)),
        out_shape=jax.ShapeDtypeStruct((t, n), out_dtype),
        scratch_shapes=[pltpu.VMEM((tm, d), BF16)],
        compiler_params=pltpu.CompilerParams(
            dimension_semantics=("parallel", "arbitrary"), vmem_limit_bytes=VMEM_LIMIT),
        name="inproj",
    )(x, norm_w, w)


def _chunk_constants():
    c = CHUNK
    t = np.arange(c)[:, None]
    u = np.arange(c)[None, :]
    mats = []
    for l in range(MXU_LEVELS):
        h = 1 << l
        mid = (t // (2 * h)) * (2 * h) + h - 1
        upper = (t & h) != 0
        mats.append(np.where(upper, (u > mid) & (u <= t), (u > t) & (u <= mid)))
    mats.append(u <= t)
    mall = np.concatenate(mats, axis=0).astype(np.float32)
    mall = np.concatenate([mall, mall], axis=1)
    x = t ^ u
    lev = np.where(t > u, np.floor(np.log2(np.maximum(x, 1))).astype(np.int32), -1)
    lev = np.where(t == u, LEVELS, lev).astype(np.int32)
    return mall, lev


def _level_exponents(cum, l):
    h = 1 << l
    parts = []
    for base in range(0, CHUNK, 2 * h):
        mid = cum[base + h - 1:base + h, :]
        parts.append(mid - cum[base:base + h])
        parts.append(cum[base + h:base + 2 * h] - mid)
    return jnp.concatenate(parts, axis=0)


def _chunks_parallel(qs, ks, as_, vs, mall, lev):
    c = CHUNK
    n = len(qs)
    dv = vs[0].shape[1]
    pieces = []
    for a in as_:
        a_hi = a.astype(BF16)
        a_lo = (a - a_hi.astype(F32)).astype(BF16)
        pieces.append(jnp.concatenate([a_hi, a_lo], axis=0))
    e_all = jnp.dot(mall, jnp.concatenate(pieces, axis=1), preferred_element_type=F32)
    es = [e_all[:, i * LANES:(i + 1) * LANES] for i in range(n)]
    cums = [e[MXU_LEVELS * c:] for e in es]

    row = lax.broadcasted_iota(jnp.int32, (c, LANES), 0)
    scores = [jnp.where(lev == LEVELS, jnp.sum(q * k, axis=1, keepdims=True), 0.0) for q, k in zip(qs, ks)]
    for l in range(LEVELS):
        upper = (row & (1 << l)) != 0
        member = lev == l
        els = [e[l * c:(l + 1) * c] if l < MXU_LEVELS else _level_exponents(cum, l) for e, cum in zip(es, cums)]
        xs = [(jnp.where(upper, q, k) * jnp.exp2(el)).astype(BF16) for q, k, el in zip(qs, ks, els)]
        scs = [lax.dot_general(x, x, (((1,), (1,)), ((), ())), preferred_element_type=F32) for x in xs]
        scores = [jnp.where(member, sc, s) for sc, s in zip(scs, scores)]

    tots = [cum[c - 1:c, :] for cum in cums]
    qcs = [(q * jnp.exp2(cum)).astype(BF16) for q, cum in zip(qs, cums)]
    krs = [(k * jnp.exp2(tot - cum)).astype(BF16) for k, tot, cum in zip(ks, tots, cums)]
    ovs = [jnp.dot(s.astype(BF16), v, preferred_element_type=F32) for s, v in zip(scores, vs)]
    us = [lax.dot_general(kr, v, (((0,), (0,)), ((), ())), preferred_element_type=F32) for kr, v in zip(krs, vs)]
    dcols = [jnp.broadcast_to(jnp.exp2(tot), (LANES, LANES)).T for tot in tots]
    if dv > LANES:
        dcols = [jnp.concatenate([d] * (dv // LANES), axis=1) for d in dcols]
    return qcs, ovs, us, dcols


def _pad_rows(x, rows):
    if x.shape[0] == rows:
        return x
    return jnp.concatenate([x, jnp.zeros((rows - x.shape[0], x.shape[1]), x.dtype)], axis=0)


def _head_out(o, norm_w, gate):
    ms = jnp.mean(o * o, axis=-1, keepdims=True)
    y = o * lax.rsqrt(ms + EPS) * norm_w
    g = gate.astype(F32)
    return (y * (g * _sigmoid(g))).astype(BF16)


def _recurrence_block(load_chunk, store_chunk, s_scr, mall_ref, lev_ref, bb, tb):
    rows = min(tb, CHUNK)
    items = [(b, r0) for b in range(bb) for r0 in range(0, tb, rows)]
    loaded = [load_chunk(b, r0, rows) for b, r0 in items]
    qs, ks, as_, vs = ([_pad_rows(ld[i], CHUNK) for ld in loaded] for i in range(4))
    qcs, ovs, us, dcols = _chunks_parallel(qs, ks, as_, vs, mall_ref[...], lev_ref[...])
    states = []
    for i, (b, r0) in enumerate(items):
        s = s_scr[b] if r0 == 0 else s
        states.append(s)
        s = dcols[i] * s + us[i]
        if r0 + rows == tb:
            s_scr[b] = s
    outs = [jnp.dot(qc, s.astype(BF16), preferred_element_type=F32) + ov for qc, s, ov in zip(qcs, states, ovs)]
    for (b, r0), o, ld in zip(items, outs, loaded):
        store_chunk(b, r0, rows, o[:rows], ld[4])


def _hgrn_kernel(hq_ref, hf_ref, hi_ref, hg_ref, s0_ref, lbl_ref, nw_ref, mall_ref, lev_ref,
                 o_ref, sout_ref, s_scr, *, bb, tb):
    ti = pl.program_id(2)

    @pl.when(ti == 0)
    def _():
        s_scr[...] = s0_ref[...]

    l0 = lbl_ref[0:1, :]
    l1 = lbl_ref[1:2, :]
    mx = jnp.maximum(l0, l1)
    e0 = jnp.exp(l0 - mx)
    e1 = jnp.exp(l1 - mx)
    lb = e0 / (e0 + e1)

    def load_chunk(b, r0, rows):
        sl = slice(r0, r0 + rows)
        zq = hq_ref[b, sl, :].astype(F32)
        q = zq * _sigmoid(zq) * (HG_DK ** -0.5)
        z = hf_ref[b, sl, :]
        ez = jnp.exp(-jnp.abs(z))
        big = 1.0 / (1.0 + ez)
        small = ez * big
        pos = z >= 0.0
        k = (1.0 - lb) * jnp.where(pos, small, big)
        a = jnp.log(lb + (1.0 - lb) * jnp.where(pos, big, small)) * LOG2E
        return q, k, a, hi_ref[b, sl, :], hg_ref[b, sl, :]

    def store_chunk(b, r0, rows, o, gate):
        o_ref[b, r0:r0 + rows, :] = _head_out(o, nw_ref[...], gate)

    _recurrence_block(load_chunk, store_chunk, s_scr, mall_ref, lev_ref, bb, tb)

    @pl.when(ti == pl.num_programs(2) - 1)
    def _():
        sout_ref[...] = s_scr[...]


def _gla_kernel(gq_ref, gk_ref, gv_ref, gg_ref, gr_ref, w2_ref, b2_ref, s0_ref, nw_ref, mall_ref, lev_ref,
                o_ref, sout_ref, s_scr, *, bb, tb):
    ti = pl.program_id(2)

    @pl.when(ti == 0)
    def _():
        s_scr[...] = s0_ref[...]

    def load_chunk(b, r0, rows):
        sl = slice(r0, r0 + rows)
        q = gq_ref[b, sl, :].astype(F32) * (GLA_DK ** -0.5)
        k = gk_ref[b, sl, :].astype(F32)
        x = jnp.dot(gr_ref[b, sl, :].astype(BF16), w2_ref[...], preferred_element_type=F32) + b2_ref[...]
        x2 = x * LOG2E
        a = (jnp.minimum(x2, 0.0) - jnp.log(1.0 + jnp.exp2(-jnp.abs(x2))) * LOG2E) * (1.0 / GLA_GATE_NORMALIZER)
        return q, k, a, gv_ref[b, sl, :], gg_ref[b, sl, :]

    def store_chunk(b, r0, rows, o, gate):
        o_ref[b, r0:r0 + rows, :] = _head_out(o, nw_ref[...], gate)

    _recurrence_block(load_chunk, store_chunk, s_scr, mall_ref, lev_ref, bb, tb)

    @pl.when(ti == pl.num_programs(2) - 1)
    def _():
        sout_ref[...] = s_scr[...]


def _rec_blocks(t_len, batch):
    tb = min(t_len, 8 * CHUNK)
    bb = 1 if t_len > CHUNK else min(batch, 8)
    return bb, tb


def _hgrn(pa, pb, s0, lb_logits, norm_w, mall, lev):
    b, t, _ = pa.shape
    bb, tb = _rec_blocks(t, b)
    blk = lambda off: pl.BlockSpec((bb, tb, LANES), lambda bi, h, ti: (bi, ti, off + h))
    const = lambda shape: pl.BlockSpec(shape, lambda bi, h, ti: (0,) * len(shape))
    st = pl.BlockSpec((bb, None, HG_DK, HG_DV), lambda bi, h, ti: (bi, h, 0, 0))
    return pl.pallas_call(
        functools.partial(_hgrn_kernel, bb=bb, tb=tb),
        grid=(b // bb, HG_HEADS, t // tb),
        in_specs=[blk(0), blk(0), blk(8), blk(16), st,
                  pl.BlockSpec((2, LANES), lambda bi, h, ti: (0, h)),
                  const((1, HG_DV)), const(mall.shape), const(lev.shape)],
        out_specs=[blk(0), st],
        out_shape=[jax.ShapeDtypeStruct((b, t, HG_HEADS * HG_DV), BF16),
                   jax.ShapeDtypeStruct(s0.shape, F32)],
        scratch_shapes=[pltpu.VMEM((bb, HG_DK, HG_DV), F32)],
        compiler_params=pltpu.CompilerParams(
            dimension_semantics=("parallel", "parallel", "arbitrary"), vmem_limit_bytes=VMEM_LIMIT),
        name="hgrn",
    )(pa, pb, pa, pa, s0, lb_logits, norm_w, mall, lev)


def _gla(pa, pb, s0, w2, b2, norm_w, mall, lev):
    b, t, _ = pa.shape
    bb, tb = _rec_blocks(t, b)
    blk = lambda width, off: pl.BlockSpec((bb, tb, width), lambda bi, h, ti: (bi, ti, off + h))
    const = lambda shape: pl.BlockSpec(shape, lambda bi, h, ti: (0,) * len(shape))
    st = pl.BlockSpec((bb, None, GLA_DK, GLA_DV), lambda bi, h, ti: (bi, h, 0, 0))
    return pl.pallas_call(
        functools.partial(_gla_kernel, bb=bb, tb=tb),
        grid=(b // bb, GLA_HEADS, t // tb),
        in_specs=[blk(LANES, 24), blk(LANES, 28), blk(GLA_DV, 16), blk(GLA_DV, 20),
                  pl.BlockSpec((bb, tb, LANES), lambda bi, h, ti: (bi, ti, 8)),
                  pl.BlockSpec((LANES, LANES), lambda bi, h, ti: (0, h)),
                  pl.BlockSpec((1, LANES), lambda bi, h, ti: (0, h)),
                  st, const((1, GLA_DV)), const(mall.shape), const(lev.shape)],
        out_specs=[blk(GLA_DV, 0), st],
        out_shape=[jax.ShapeDtypeStruct((b, t, GLA_HEADS * GLA_DV), BF16),
                   jax.ShapeDtypeStruct(s0.shape, F32)],
        scratch_shapes=[pltpu.VMEM((bb, GLA_DK, GLA_DV), F32)],
        compiler_params=pltpu.CompilerParams(
            dimension_semantics=("parallel", "parallel", "arbitrary"), vmem_limit_bytes=VMEM_LIMIT),
        name="gla",
    )(pa, pa, pa, pa, pb, w2, b2, s0, norm_w, mall, lev)


def _outproj_kernel(ohg_ref, ogla_ref, wt_ref, wb_ref, x_ref, nw_ref, x1_ref, h2_ref):
    m = (jnp.dot(ohg_ref[...], wt_ref[...], preferred_element_type=F32)
         + jnp.dot(ogla_ref[...], wb_ref[...], preferred_element_type=F32))
    x1 = x_ref[...] + m
    x1_ref[...] = x1
    h2_ref[...] = _rms(x1, nw_ref[...]).astype(BF16)


def _outproj(o_hg, o_gla, w_top, w_bot, x, norm_w, tm):
    t, d = x.shape
    half = o_hg.shape[1]
    return pl.pallas_call(
        _outproj_kernel,
        grid=(t // tm,),
        in_specs=[
            pl.BlockSpec((tm, half), lambda i: (i, 0)),
            pl.BlockSpec((tm, half), lambda i: (i, 0)),
            pl.BlockSpec((half, d), lambda i: (0, 0)),
            pl.BlockSpec((half, d), lambda i: (0, 0)),
            pl.BlockSpec((tm, d), lambda i: (i, 0)),
            pl.BlockSpec((1, d), lambda i: (0, 0)),
        ],
        out_specs=[pl.BlockSpec((tm, d), lambda i: (i, 0)), pl.BlockSpec((tm, d), lambda i: (i, 0))],
        out_shape=[jax.ShapeDtypeStruct((t, d), F32), jax.ShapeDtypeStruct((t, d), BF16)],
        compiler_params=pltpu.CompilerParams(
            dimension_semantics=("parallel",), vmem_limit_bytes=VMEM_LIMIT),
        name="outproj",
    )(o_hg, o_gla, w_top, w_bot, x, norm_w)


def _ffn_kernel(h_ref, wg_ref, wu_ref, wd_ref, x1_ref, nw_ref, o_ref, acc_scr):
    j = pl.program_id(1)
    h = h_ref[...]
    g = jnp.dot(h, wg_ref[...], preferred_element_type=F32)
    u = jnp.dot(h, wu_ref[...], preferred_element_type=F32)
    mid = ((g * _sigmoid(g)) * u).astype(BF16)
    d = jnp.dot(mid, wd_ref[...], preferred_element_type=F32)

    @pl.when(j == 0)
    def _():
        acc_scr[...] = d

    @pl.when(j > 0)
    def _():
        acc_scr[...] += d

    @pl.when(j == pl.num_programs(1) - 1)
    def _():
        o_ref[...] = _rms(x1_ref[...] + acc_scr[...], nw_ref[...])


def _ffn(h2, w_gate_up, w_down, x1, norm_w, tm, tf):
    t, d = x1.shape
    nf = D_FF // tf
    return pl.pallas_call(
        _ffn_kernel,
        grid=(t // tm, nf),
        in_specs=[
            pl.BlockSpec((tm, d), lambda i, j: (i, 0)),
            pl.BlockSpec((d, tf), lambda i, j: (0, j)),
            pl.BlockSpec((d, tf), lambda i, j: (0, nf + j)),
            pl.BlockSpec((tf, d), lambda i, j: (j, 0)),
            pl.BlockSpec((tm, d), lambda i, j: (i, 0)),
            pl.BlockSpec((1, d), lambda i, j: (0, 0)),
        ],
        out_specs=pl.BlockSpec((tm, d), lambda i, j: (i, 0)),
        out_shape=jax.ShapeDtypeStruct((t, d), F32),
        scratch_shapes=[pltpu.VMEM((tm, d), F32)],
        compiler_params=pltpu.CompilerParams(
            dimension_semantics=("parallel", "arbitrary"), vmem_limit_bytes=VMEM_LIMIT),
        name="ffn",
    )(h2, w_gate_up, w_gate_up, w_down, x1, norm_w)


def _trunk(x, s_hg, s_gla, wts, consts):
    b, t, d = x.shape
    xf = x.reshape(b * t, d)
    mall, lev = consts
    pa = _inproj(xf, wts["norm_mix"], wts["w_a"], BF16, 1024, 1024).reshape(b, t, PA_COLS)
    pb = _inproj(xf, wts["norm_mix"], wts["w_b"], F32, 1024, PB_COLS).reshape(b, t, PB_COLS)
    o_hg, s_hg_new = _hgrn(pa, pb, s_hg, wts["lb_logits"], wts["hg_norm"], mall, lev)
    o_gla, s_gla_new = _gla(pa, pb, s_gla, wts["w_gk2"], wts["b_gk"], wts["gla_norm"], mall, lev)
    x1, h2 = _outproj(o_hg.reshape(b * t, -1), o_gla.reshape(b * t, -1), wts["w_out_top"], wts["w_out_bot"],
                      xf, wts["norm_ffn"], 512)
    y = _ffn(h2, wts["w_gate_up"], wts["w_down"], x1, wts["norm_final"], 512, 512)
    return y.reshape(b, t, d), s_hg_new[None], s_gla_new[None]


def kernel(x_prompt, x_sample, state_hgrn, state_gla, lb_logits, norm_mix, w_in, w_gk2, b_gk, hg_norm, gla_norm,
           w_out, norm_ffn, w_gate_up, w_down, norm_final):
    w = w_in[0]
    hw = HG_HEADS * HG_DK
    w_a = jnp.concatenate([w[:, 0:hw], w[:, 2 * hw:4 * hw], w[:, 4 * hw:4 * hw + 3072]], axis=1).astype(BF16)
    w_b = jnp.concatenate([w[:, hw:2 * hw], w[:, N_GR:N_GR + GLA_RANK],
                           jnp.zeros((D_MODEL, LANES - GLA_RANK), F32)], axis=1).astype(BF16)
    w2 = jnp.concatenate([w_gk2[0], jnp.zeros((LANES - GLA_RANK, GLA_HEADS * GLA_DK), F32)], axis=0).astype(BF16)
    wo = w_out[0].astype(BF16)
    wts = {
        "norm_mix": norm_mix[0][None, :], "w_a": w_a, "w_b": w_b,
        "lb_logits": lb_logits, "hg_norm": hg_norm[0][None, :],
        "w_gk2": w2, "b_gk": b_gk[0][None, :], "gla_norm": gla_norm[0][None, :],
        "w_out_top": wo[:hw], "w_out_bot": wo[hw:], "norm_ffn": norm_ffn[0][None, :],
        "w_gate_up": w_gate_up[0].astype(BF16), "w_down": w_down[0].astype(BF16),
        "norm_final": norm_final[None, :],
    }
    mall_np, lev_np = _chunk_constants()
    consts = (jnp.asarray(mall_np, BF16), jnp.asarray(lev_np))

    bp = x_prompt.shape[0]
    zero_hg = jnp.zeros((bp, HG_HEADS, HG_DK, HG_DV), F32)
    zero_gla = jnp.zeros((bp, GLA_HEADS, GLA_DK, GLA_DV), F32)
    y_p, hg_p, gla_p = _trunk(x_prompt, zero_hg, zero_gla, wts, consts)
    y_s, hg_s, gla_s = _trunk(x_sample, state_hgrn[0], state_gla[0], wts, consts)
    return (y_p, y_s, hg_p, gla_p, hg_s, gla_s)


N_GR = 4 * HG_HEADS * HG_DK + 3072
```

```python
import functools

import numpy as np
import jax
import jax.numpy as jnp
from jax import lax
from jax.experimental import pallas as pl
from jax.experimental.pallas import tpu as pltpu

F32 = jnp.float32
BF16 = jnp.bfloat16

D_MODEL = 2048
HG_HEADS = 8
HG_DK = 128
HG_DV = 128
GLA_HEADS = 4
GLA_DK = 128
GLA_DV = 256
GLA_RANK = 16
GLA_GATE_NORMALIZER = 16.0
D_FF = 5632
EPS = 1e-6
LOG2E = 1.4426950408889634

LANES = 128
CHUNK = 128
LEVELS = 7
MXU_LEVELS = 3
CHUNKS_PER_STEP = 16
VMEM_LIMIT = 56 * 1024 * 1024

PA_COLS = 6144
PB_COLS = 1152


def _rms(x, w):
    ms = jnp.mean(x * x, axis=-1, keepdims=True)
    return (x * lax.rsqrt(ms + EPS)) * w


def _sigmoid(x):
    return 1.0 / (1.0 + jnp.exp(-x))


def _inproj_kernel(x_ref, nw_ref, w_ref, o_ref, h_scr):
    @pl.when(pl.program_id(1) == 0)
    def _():
        h_scr[...] = _rms(x_ref[...], nw_ref[...]).astype(BF16)

    o_ref[...] = jnp.dot(h_scr[...], w_ref[...], preferred_element_type=F32).astype(o_ref.dtype)


def _inproj(x, norm_w, w, out_dtype, tm, tn):
    t, d = x.shape
    n = w.shape[1]
    return pl.pallas_call(
        _inproj_kernel,
        grid=(t // tm, n // tn),
        in_specs=[
            pl.BlockSpec((tm, d), lambda i, j: (i, 0)),
            pl.BlockSpec((1, d), lambda i, j: (0, 0)),
            pl.BlockSpec((d, tn), lambda i, j: (0, j)),
        ],
        out_specs=pl.BlockSpec((tm, tn), lambda i, j: (i, j)),
        out_shape=jax.ShapeDtypeStruct((t, n), out_dtype),
        scratch_shapes=[pltpu.VMEM((tm, d), BF16)],
        compiler_params=pltpu.CompilerParams(
            dimension_semantics=("parallel", "arbitrary"), vmem_limit_bytes=VMEM_LIMIT),
        name="inproj",
    )(x, norm_w, w)


def _chunk_constants():
    c = CHUNK
    t = np.arange(c)[:, None]
    u = np.arange(c)[None, :]
    mats = []
    for l in range(MXU_LEVELS):
        h = 1 << l
        mid = (t // (2 * h)) * (2 * h) + h - 1
        upper = (t & h) != 0
        mats.append(np.where(upper, (u > mid) & (u <= t), (u > t) & (u <= mid)))
    mats.append(u <= t)
    mall = np.concatenate(mats, axis=0).astype(np.float32)
    mall = np.concatenate([mall, mall], axis=1)
    x = t ^ u
    lev = np.where(t > u, np.floor(np.log2(np.maximum(x, 1))).astype(np.int32), -1)
    lev = np.where(t == u, LEVELS, lev).astype(np.int32)
    return mall, lev


def _aligned_operands(q, k, cum, l):
    h = 1 << l
    lows, ups = [], []
    for base in range(0, CHUNK, 2 * h):
        mid = cum[base + h - 1:base + h, :]
        lows.append(k[base:base + h] * jnp.exp2(mid - cum[base:base + h]))
        ups.append(q[base + h:base + 2 * h] * jnp.exp2(cum[base + h:base + 2 * h] - mid))
    x_up = jnp.concatenate(ups, axis=0).astype(BF16)
    x = jnp.concatenate([z for pair in zip(lows, ups) for z in pair], axis=0).astype(BF16)
    return x_up, x


def _aligned_update(scores, sc, member, l):
    h = 1 << l
    parts = []
    for i, base in enumerate(range(0, CHUNK, 2 * h)):
        rows = slice(base + h, base + 2 * h)
        parts.append(scores[base:base + h])
        parts.append(jnp.where(member[rows], sc[i * h:(i + 1) * h], scores[rows]))
    return jnp.concatenate(parts, axis=0)


def _chunks_parallel(qs, ks, as_, vs, mall, lev):
    c = CHUNK
    n = len(qs)
    dv = vs[0].shape[1]
    pieces = []
    for a in as_:
        a_hi = a.astype(BF16)
        a_lo = (a - a_hi.astype(F32)).astype(BF16)
        pieces.append(jnp.concatenate([a_hi, a_lo], axis=0))
    e_all = jnp.dot(mall, jnp.concatenate(pieces, axis=1), preferred_element_type=F32)
    es = [e_all[:, i * LANES:(i + 1) * LANES] for i in range(n)]
    cums = [e[MXU_LEVELS * c:] for e in es]

    row = lax.broadcasted_iota(jnp.int32, (c, LANES), 0)
    scores = [jnp.where(lev == LEVELS, jnp.sum(q * k, axis=1, keepdims=True), 0.0) for q, k in zip(qs, ks)]
    for l in range(MXU_LEVELS):
        upper = (row & (1 << l)) != 0
        member = lev == l
        xs = [(jnp.where(upper, q, k) * jnp.exp2(e[l * c:(l + 1) * c])).astype(BF16) for q, k, e in zip(qs, ks, es)]
        scs = [lax.dot_general(x, x, (((1,), (1,)), ((), ())), preferred_element_type=F32) for x in xs]
        scores = [jnp.where(member, sc, s) for sc, s in zip(scs, scores)]
    for l in range(MXU_LEVELS, LEVELS):
        member = lev == l
        ops = [_aligned_operands(q, k, cum, l) for q, k, cum in zip(qs, ks, cums)]
        scs = [lax.dot_general(x_up, x, (((1,), (1,)), ((), ())), preferred_element_type=F32) for x_up, x in ops]
        scores = [_aligned_update(s, sc, member, l) for s, sc in zip(scores, scs)]

    tots = [cum[c - 1:c, :] for cum in cums]
    qcs = [(q * jnp.exp2(cum)).astype(BF16) for q, cum in zip(qs, cums)]
    krs = [(k * jnp.exp2(tot - cum)).astype(BF16) for k, tot, cum in zip(ks, tots, cums)]
    ovs = [jnp.dot(s.astype(BF16), v, preferred_element_type=F32) for s, v in zip(scores, vs)]
    us = [lax.dot_general(kr, v, (((0,), (0,)), ((), ())), preferred_element_type=F32) for kr, v in zip(krs, vs)]
    dcols = [jnp.broadcast_to(jnp.exp2(tot), (LANES, LANES)).T for tot in tots]
    if dv > LANES:
        dcols = [jnp.concatenate([d] * (dv // LANES), axis=1) for d in dcols]
    return qcs, ovs, us, dcols


def _pad_rows(x, rows):
    if x.shape[0] == rows:
        return x
    return jnp.concatenate([x, jnp.zeros((rows - x.shape[0], x.shape[1]), x.dtype)], axis=0)


def _head_out(o, norm_w, gate):
    ms = jnp.mean(o * o, axis=-1, keepdims=True)
    y = o * lax.rsqrt(ms + EPS) * norm_w
    g = gate.astype(F32)
    return (y * (g * _sigmoid(g))).astype(BF16)


def _recurrence_block(load_chunk, store_chunk, s_scr, mall_ref, lev_ref, bb, tb):
    rows = min(tb, CHUNK)
    items = [(b, r0) for b in range(bb) for r0 in range(0, tb, rows)]
    loaded = [load_chunk(b, r0, rows) for b, r0 in items]
    qs, ks, as_, vs = ([_pad_rows(ld[i], CHUNK) for ld in loaded] for i in range(4))
    qcs, ovs, us, dcols = _chunks_parallel(qs, ks, as_, vs, mall_ref[...], lev_ref[...])
    states = []
    for i, (b, r0) in enumerate(items):
        s = s_scr[b] if r0 == 0 else s
        states.append(s)
        s = dcols[i] * s + us[i]
        if r0 + rows == tb:
            s_scr[b] = s
    outs = [jnp.dot(qc, s.astype(BF16), preferred_element_type=F32) + ov for qc, s, ov in zip(qcs, states, ovs)]
    for (b, r0), o, ld in zip(items, outs, loaded):
        store_chunk(b, r0, rows, o[:rows], ld[4])


def _hgrn_kernel(hq_ref, hf_ref, hi_ref, hg_ref, s0_ref, lbl_ref, nw_ref, mall_ref, lev_ref,
                 o_ref, sout_ref, s_scr, *, bb, tb):
    ti = pl.program_id(2)

    @pl.when(ti == 0)
    def _():
        s_scr[...] = s0_ref[...]

    l0 = lbl_ref[0:1, :]
    l1 = lbl_ref[1:2, :]
    mx = jnp.maximum(l0, l1)
    e0 = jnp.exp(l0 - mx)
    e1 = jnp.exp(l1 - mx)
    lb = e0 / (e0 + e1)

    def load_chunk(b, r0, rows):
        sl = slice(r0, r0 + rows)
        zq = hq_ref[b, sl, :].astype(F32)
        q = zq * _sigmoid(zq) * (HG_DK ** -0.5)
        z = hf_ref[b, sl, :]
        ez = jnp.exp(-jnp.abs(z))
        big = 1.0 / (1.0 + ez)
        small = ez * big
        pos = z >= 0.0
        k = (1.0 - lb) * jnp.where(pos, small, big)
        a = jnp.log(lb + (1.0 - lb) * jnp.where(pos, big, small)) * LOG2E
        return q, k, a, hi_ref[b, sl, :], hg_ref[b, sl, :]

    def store_chunk(b, r0, rows, o, gate):
        o_ref[b, r0:r0 + rows, :] = _head_out(o, nw_ref[...], gate)

    _recurrence_block(load_chunk, store_chunk, s_scr, mall_ref, lev_ref, bb, tb)

    @pl.when(ti == pl.num_programs(2) - 1)
    def _():
        sout_ref[...] = s_scr[...]


def _gla_kernel(gq_ref, gk_ref, gv_ref, gg_ref, gr_ref, w2_ref, b2_ref, s0_ref, nw_ref, mall_ref, lev_ref,
                o_ref, sout_ref, s_scr, *, bb, tb):
    ti = pl.program_id(2)

    @pl.when(ti == 0)
    def _():
        s_scr[...] = s0_ref[...]

    def load_chunk(b, r0, rows):
        sl = slice(r0, r0 + rows)
        q = gq_ref[b, sl, :].astype(F32) * (GLA_DK ** -0.5)
        k = gk_ref[b, sl, :].astype(F32)
        x = jnp.dot(gr_ref[b, sl, :].astype(BF16), w2_ref[...], preferred_element_type=F32) + b2_ref[...]
        x2 = x * LOG2E
        a = (jnp.minimum(x2, 0.0) - jnp.log(1.0 + jnp.exp2(-jnp.abs(x2))) * LOG2E) * (1.0 / GLA_GATE_NORMALIZER)
        return q, k, a, gv_ref[b, sl, :], gg_ref[b, sl, :]

    def store_chunk(b, r0, rows, o, gate):
        o_ref[b, r0:r0 + rows, :] = _head_out(o, nw_ref[...], gate)

    _recurrence_block(load_chunk, store_chunk, s_scr, mall_ref, lev_ref, bb, tb)

    @pl.when(ti == pl.num_programs(2) - 1)
    def _():
        sout_ref[...] = s_scr[...]


def _rec_blocks(t_len, batch):
    tb = min(t_len, CHUNKS_PER_STEP * CHUNK)
    bb = 1 if t_len > CHUNK else min(batch, CHUNKS_PER_STEP)
    return bb, tb


def _hgrn(pa, pb, s0, lb_logits, norm_w, mall, lev):
    b, t, _ = pa.shape
    bb, tb = _rec_blocks(t, b)
    blk = lambda off: pl.BlockSpec((bb, tb, LANES), lambda bi, h, ti: (bi, ti, off + h))
    const = lambda shape: pl.BlockSpec(shape, lambda bi, h, ti: (0,) * len(shape))
    st = pl.BlockSpec((bb, None, HG_DK, HG_DV), lambda bi, h, ti: (bi, h, 0, 0))
    return pl.pallas_call(
        functools.partial(_hgrn_kernel, bb=bb, tb=tb),
        grid=(b // bb, HG_HEADS, t // tb),
        in_specs=[blk(0), blk(0), blk(8), blk(16), st,
                  pl.BlockSpec((2, LANES), lambda bi, h, ti: (0, h)),
                  const((1, HG_DV)), const(mall.shape), const(lev.shape)],
        out_specs=[blk(0), st],
        out_shape=[jax.ShapeDtypeStruct((b, t, HG_HEADS * HG_DV), BF16),
                   jax.ShapeDtypeStruct(s0.shape, F32)],
        scratch_shapes=[pltpu.VMEM((bb, HG_DK, HG_DV), F32)],
        compiler_params=pltpu.CompilerParams(
            dimension_semantics=("parallel", "parallel", "arbitrary"), vmem_limit_bytes=VMEM_LIMIT),
        name="hgrn",
    )(pa, pb, pa, pa, s0, lb_logits, norm_w, mall, lev)


def _gla(pa, pb, s0, w2, b2, norm_w, mall, lev):
    b, t, _ = pa.shape
    bb, tb = _rec_blocks(t, b)
    blk = lambda width, off: pl.BlockSpec((bb, tb, width), lambda bi, h, ti: (bi, ti, off + h))
    const = lambda shape: pl.BlockSpec(shape, lambda bi, h, ti: (0,) * len(shape))
    st = pl.BlockSpec((bb, None, GLA_DK, GLA_DV), lambda bi, h, ti: (bi, h, 0, 0))
    return pl.pallas_call(
        functools.partial(_gla_kernel, bb=bb, tb=tb),
        grid=(b // bb, GLA_HEADS, t // tb),
        in_specs=[blk(LANES, 24), blk(LANES, 28), blk(GLA_DV, 16), blk(GLA_DV, 20),
                  pl.BlockSpec((bb, tb, LANES), lambda bi, h, ti: (bi, ti, 8)),
                  pl.BlockSpec((LANES, LANES), lambda bi, h, ti: (0, h)),
                  pl.BlockSpec((1, LANES), lambda bi, h, ti: (0, h)),
                  st, const((1, GLA_DV)), const(mall.shape), const(lev.shape)],
        out_specs=[blk(GLA_DV, 0), st],
        out_shape=[jax.ShapeDtypeStruct((b, t, GLA_HEADS * GLA_DV), BF16),
                   jax.ShapeDtypeStruct(s0.shape, F32)],
        scratch_shapes=[pltpu.VMEM((bb, GLA_DK, GLA_DV), F32)],
        compiler_params=pltpu.CompilerParams(
            dimension_semantics=("parallel", "parallel", "arbitrary"), vmem_limit_bytes=VMEM_LIMIT),
        name="gla",
    )(pa, pa, pa, pa, pb, w2, b2, s0, norm_w, mall, lev)


def _outproj_kernel(ohg_ref, ogla_ref, wt_ref, wb_ref, x_ref, nw_ref, x1_ref, h2_ref):
    m = (jnp.dot(ohg_ref[...], wt_ref[...], preferred_element_type=F32)
         + jnp.dot(ogla_ref[...], wb_ref[...], preferred_element_type=F32))
    x1 = x_ref[...] + m
    x1_ref[...] = x1
    h2_ref[...] = _rms(x1, nw_ref[...]).astype(BF16)


def _outproj(o_hg, o_gla, w_top, w_bot, x, norm_w, tm):
    t, d = x.shape
    half = o_hg.shape[1]
    return pl.pallas_call(
        _outproj_kernel,
        grid=(t // tm,),
        in_specs=[
            pl.BlockSpec((tm, half), lambda i: (i, 0)),
            pl.BlockSpec((tm, half), lambda i: (i, 0)),
            pl.BlockSpec((half, d), lambda i: (0, 0)),
            pl.BlockSpec((half, d), lambda i: (0, 0)),
            pl.BlockSpec((tm, d), lambda i: (i, 0)),
            pl.BlockSpec((1, d), lambda i: (0, 0)),
        ],
        out_specs=[pl.BlockSpec((tm, d), lambda i: (i, 0)), pl.BlockSpec((tm, d), lambda i: (i, 0))],
        out_shape=[jax.ShapeDtypeStruct((t, d), F32), jax.ShapeDtypeStruct((t, d), BF16)],
        compiler_params=pltpu.CompilerParams(
            dimension_semantics=("parallel",), vmem_limit_bytes=VMEM_LIMIT),
        name="outproj",
    )(o_hg, o_gla, w_top, w_bot, x, norm_w)


def _ffn_kernel(h_ref, wg_ref, wu_ref, wd_ref, x1_ref, nw_ref, o_ref):
    j = pl.program_id(1)

    @pl.when(j == 0)
    def _():
        o_ref[...] = x1_ref[...]

    h = h_ref[...]
    g = jnp.dot(h, wg_ref[...], preferred_element_type=F32)
    u = jnp.dot(h, wu_ref[...], preferred_element_type=F32)
    mid = ((g * _sigmoid(g)) * u).astype(BF16)
    o_ref[...] += jnp.dot(mid, wd_ref[...], preferred_element_type=F32)

    @pl.when(j == pl.num_programs(1) - 1)
    def _():
        o_ref[...] = _rms(o_ref[...], nw_ref[...])


def _ffn(h2, w_gate_up, w_down, x1, norm_w, tm, tf):
    t, d = x1.shape
    nf = D_FF // tf
    return pl.pallas_call(
        _ffn_kernel,
        grid=(t // tm, nf),
        in_specs=[
            pl.BlockSpec((tm, d), lambda i, j: (i, 0)),
            pl.BlockSpec((d, tf), lambda i, j: (0, j)),
            pl.BlockSpec((d, tf), lambda i, j: (0, nf + j)),
            pl.BlockSpec((tf, d), lambda i, j: (j, 0)),
            pl.BlockSpec((tm, d), lambda i, j: (i, 0)),
            pl.BlockSpec((1, d), lambda i, j: (0, 0)),
        ],
        out_specs=pl.BlockSpec((tm, d), lambda i, j: (i, 0)),
        out_shape=jax.ShapeDtypeStruct((t, d), F32),
        compiler_params=pltpu.CompilerParams(
            dimension_semantics=("parallel", "arbitrary"), vmem_limit_bytes=VMEM_LIMIT),
        name="ffn",
    )(h2, w_gate_up, w_gate_up, w_down, x1, norm_w)


def _trunk(x, s_hg, s_gla, wts, consts):
    b, t, d = x.shape
    xf = x.reshape(b * t, d)
    mall, lev = consts
    pa = _inproj(xf, wts["norm_mix"], wts["w_a"], BF16, 1024, 1024).reshape(b, t, PA_COLS)
    pb = _inproj(xf, wts["norm_mix"], wts["w_b"], F32, 1024, PB_COLS).reshape(b, t, PB_COLS)
    o_hg, s_hg_new = _hgrn(pa, pb, s_hg, wts["lb_logits"], wts["hg_norm"], mall, lev)
    o_gla, s_gla_new = _gla(pa, pb, s_gla, wts["w_gk2"], wts["b_gk"], wts["gla_norm"], mall, lev)
    x1, h2 = _outproj(o_hg.reshape(b * t, -1), o_gla.reshape(b * t, -1), wts["w_out_top"], wts["w_out_bot"],
                      xf, wts["norm_ffn"], 512)
    y = _ffn(h2, wts["w_gate_up"], wts["w_down"], x1, wts["norm_final"], 512, 512)
    return y.reshape(b, t, d), s_hg_new[None], s_gla_new[None]


def kernel(x_prompt, x_sample, state_hgrn, state_gla, lb_logits, norm_mix, w_in, w_gk2, b_gk, hg_norm, gla_norm,
           w_out, norm_ffn, w_gate_up, w_down, norm_final):
    w = w_in[0]
    hw = HG_HEADS * HG_DK
    w_a = jnp.concatenate([w[:, 0:hw], w[:, 2 * hw:4 * hw], w[:, 4 * hw:4 * hw + 3072]], axis=1).astype(BF16)
    w_b = jnp.concatenate([w[:, hw:2 * hw], w[:, N_GR:N_GR + GLA_RANK],
                           jnp.zeros((D_MODEL, LANES - GLA_RANK), F32)], axis=1).astype(BF16)
    w2 = jnp.concatenate([w_gk2[0], jnp.zeros((LANES - GLA_RANK, GLA_HEADS * GLA_DK), F32)], axis=0).astype(BF16)
    wo = w_out[0].astype(BF16)
    wts = {
        "norm_mix": norm_mix[0][None, :], "w_a": w_a, "w_b": w_b,
        "lb_logits": lb_logits, "hg_norm": hg_norm[0][None, :],
        "w_gk2": w2, "b_gk": b_gk[0][None, :], "gla_norm": gla_norm[0][None, :],
        "w_out_top": wo[:hw], "w_out_bot": wo[hw:], "norm_ffn": norm_ffn[0][None, :],
        "w_gate_up": w_gate_up[0].astype(BF16), "w_down": w_down[0].astype(BF16),
        "norm_final": norm_final[None, :],
    }
    mall_np, lev_np = _chunk_constants()
    consts = (jnp.asarray(mall_np, BF16), jnp.asarray(lev_np))

    bp = x_prompt.shape[0]
    zero_hg = jnp.zeros((bp, HG_HEADS, HG_DK, HG_DV), F32)
    zero_gla = jnp.zeros((bp, GLA_HEADS, GLA_DK, GLA_DV), F32)
    y_p, hg_p, gla_p = _trunk(x_prompt, zero_hg, zero_gla, wts, consts)
    y_s, hg_s, gla_s = _trunk(x_sample, state_hgrn[0], state_gla[0], wts, consts)
    return (y_p, y_s, hg_p, gla_p, hg_s, gla_s)


N_GR = 4 * HG_HEADS * HG_DK + 3072
```

```python
import functools

import numpy as np
import jax
import jax.numpy as jnp
from jax import lax
from jax.experimental import pallas as pl
from jax.experimental.pallas import tpu as pltpu

F32 = jnp.float32
BF16 = jnp.bfloat16

D_MODEL = 2048
HG_HEADS = 8
HG_DK = 128
HG_DV = 128
GLA_HEADS = 4
GLA_DK = 128
GLA_DV = 256
GLA_RANK = 16
GLA_GATE_NORMALIZER = 16.0
D_FF = 5632
EPS = 1e-6
LOG2E = 1.4426950408889634

LANES = 128
CHUNK = 128
LEVELS = 7
MXU_LEVELS = 3
CHUNKS_PER_STEP = 16
VMEM_LIMIT = 56 * 1024 * 1024

IN_TILE = 1024
IN_TILES = 7
IN_SUB = 256
P_COLS = IN_TILE * IN_TILES


def _rms(x, w):
    ms = jnp.mean(x * x, axis=-1, keepdims=True)
    return (x * lax.rsqrt(ms + EPS)) * w


def _sigmoid(x):
    return 1.0 / (1.0 + jnp.exp(-x))


def _silu(x):
    return x * _sigmoid(x)


def _inproj_kernel(x_ref, nw_ref, w_ref, wr_ref, lbl_ref, w2_ref, b2_ref, p_ref, ahg_ref, agla_ref, h_scr):
    j = pl.program_id(1)
    hg = HG_HEADS * HG_DK
    gl = GLA_HEADS * GLA_DK

    @pl.when(j == 0)
    def _():
        h_scr[...] = _rms(x_ref[...], nw_ref[...]).astype(BF16)

    def forget_gate(z, cols):
        l0 = lbl_ref[0:1, cols]
        l1 = lbl_ref[1:2, cols]
        mx = jnp.maximum(l0, l1)
        e0 = jnp.exp(l0 - mx)
        e1 = jnp.exp(l1 - mx)
        lb = e0 / (e0 + e1)
        ez = jnp.exp(-jnp.abs(z))
        big = 1.0 / (1.0 + ez)
        small = ez * big
        pos = z >= 0.0
        a = jnp.log(lb + (1.0 - lb) * jnp.where(pos, big, small)) * LOG2E
        hi = a.astype(BF16)
        ahg_ref[:, cols] = hi
        ahg_ref[:, hg + cols.start:hg + cols.stop] = (a - hi.astype(F32)).astype(BF16)
        return (1.0 - lb) * jnp.where(pos, small, big)

    activations = (
        lambda p, cols: _silu(p) * (HG_DK ** -0.5),
        forget_gate,
        lambda p, cols: p,
        lambda p, cols: _silu(p),
        lambda p, cols: p * ((GLA_DK ** -0.5) if cols.stop <= gl else 1.0),
        lambda p, cols: p,
        lambda p, cols: _silu(p),
    )

    def low_rank_gate():
        gr = jnp.dot(h_scr[...], wr_ref[...], preferred_element_type=F32).astype(BF16)
        x2 = (jnp.dot(gr, w2_ref[...], preferred_element_type=F32) + b2_ref[...]) * LOG2E
        a = (jnp.minimum(x2, 0.0) - jnp.log(1.0 + jnp.exp2(-jnp.abs(x2))) * LOG2E) * (1.0 / GLA_GATE_NORMALIZER)
        hi = a.astype(BF16)
        agla_ref[:, :gl] = hi
        agla_ref[:, gl:] = (a - hi.astype(F32)).astype(BF16)

    for tile in range(IN_TILES):
        @pl.when(j == tile)
        def _(tile=tile):
            for c0 in range(0, IN_TILE, IN_SUB):
                cols = slice(c0, c0 + IN_SUB)
                p = jnp.dot(h_scr[...], w_ref[:, cols], preferred_element_type=F32)
                p_ref[:, cols] = activations[tile](p, cols).astype(BF16)
            if tile == IN_TILES - 1:
                low_rank_gate()


def _inproj(x, norm_w, w, w_gr, lb_logits, w2, b2, tm):
    t, d = x.shape
    hg = HG_HEADS * HG_DK
    gl = GLA_HEADS * GLA_DK
    const = lambda shape: pl.BlockSpec(shape, lambda i, j: (0,) * len(shape))
    return pl.pallas_call(
        _inproj_kernel,
        grid=(t // tm, IN_TILES),
        in_specs=[
            pl.BlockSpec((tm, d), lambda i, j: (i, 0)),
            const((1, d)),
            pl.BlockSpec((d, IN_TILE), lambda i, j: (0, j)),
            const((d, LANES)), const((2, hg)), const((LANES, gl)), const((1, gl)),
        ],
        out_specs=[pl.BlockSpec((tm, IN_TILE), lambda i, j: (i, j)),
                   pl.BlockSpec((tm, 2 * hg), lambda i, j: (i, 0)),
                   pl.BlockSpec((tm, 2 * gl), lambda i, j: (i, 0))],
        out_shape=[jax.ShapeDtypeStruct((t, P_COLS), BF16),
                   jax.ShapeDtypeStruct((t, 2 * hg), BF16),
                   jax.ShapeDtypeStruct((t, 2 * gl), BF16)],
        scratch_shapes=[pltpu.VMEM((tm, d), BF16)],
        compiler_params=pltpu.CompilerParams(
            dimension_semantics=("parallel", "arbitrary"), vmem_limit_bytes=VMEM_LIMIT),
        name="inproj",
    )(x, norm_w, w, w_gr, lb_logits, w2, b2)


def _chunk_constants():
    c = CHUNK
    t = np.arange(c)[:, None]
    u = np.arange(c)[None, :]
    mats = []
    for l in range(MXU_LEVELS):
        h = 1 << l
        mid = (t // (2 * h)) * (2 * h) + h - 1
        upper = (t & h) != 0
        mats.append(np.where(upper, (u > mid) & (u <= t), (u > t) & (u <= mid)))
    mats.append(u <= t)
    mall = np.concatenate(mats, axis=0).astype(np.float32)
    mall = np.concatenate([mall, mall], axis=1)
    x = t ^ u
    lev = np.where(t > u, np.floor(np.log2(np.maximum(x, 1))).astype(np.int32), -1)
    lev = np.where(t == u, LEVELS, lev).astype(np.int32)
    return mall, lev


def _aligned_operands(q, k, cum, l):
    h = 1 << l
    lows, ups = [], []
    for base in range(0, CHUNK, 2 * h):
        mid = cum[base + h - 1:base + h, :]
        lows.append(k[base:base + h] * jnp.exp2(mid - cum[base:base + h]))
        ups.append(q[base + h:base + 2 * h] * jnp.exp2(cum[base + h:base + 2 * h] - mid))
    x_up = jnp.concatenate(ups, axis=0).astype(BF16)
    x = jnp.concatenate([z for pair in zip(lows, ups) for z in pair], axis=0).astype(BF16)
    return x_up, x


def _aligned_update(scores, sc, member, l):
    h = 1 << l
    parts = []
    for i, base in enumerate(range(0, CHUNK, 2 * h)):
        rows = slice(base + h, base + 2 * h)
        parts.append(scores[base:base + h])
        parts.append(jnp.where(member[rows], sc[i * h:(i + 1) * h], scores[rows]))
    return jnp.concatenate(parts, axis=0)


def _chunks_parallel(qs, ks, as_, vs, mall, lev):
    c = CHUNK
    n = len(qs)
    dv = vs[0].shape[1]
    e_all = jnp.dot(mall, jnp.concatenate(as_, axis=1), preferred_element_type=F32)
    es = [e_all[:, i * LANES:(i + 1) * LANES] for i in range(n)]
    cums = [e[MXU_LEVELS * c:] for e in es]

    row = lax.broadcasted_iota(jnp.int32, (c, LANES), 0)
    scores = [jnp.where(lev == LEVELS, jnp.sum(q * k, axis=1, keepdims=True), 0.0) for q, k in zip(qs, ks)]
    for l in range(MXU_LEVELS):
        upper = (row & (1 << l)) != 0
        member = lev == l
        xs = [(jnp.where(upper, q, k) * jnp.exp2(e[l * c:(l + 1) * c])).astype(BF16) for q, k, e in zip(qs, ks, es)]
        scs = [lax.dot_general(x, x, (((1,), (1,)), ((), ())), preferred_element_type=F32) for x in xs]
        scores = [jnp.where(member, sc, s) for sc, s in zip(scs, scores)]
    for l in range(MXU_LEVELS, LEVELS):
        member = lev == l
        ops = [_aligned_operands(q, k, cum, l) for q, k, cum in zip(qs, ks, cums)]
        scs = [lax.dot_general(x_up, x, (((1,), (1,)), ((), ())), preferred_element_type=F32) for x_up, x in ops]
        scores = [_aligned_update(s, sc, member, l) for s, sc in zip(scores, scs)]

    tots = [cum[c - 1:c, :] for cum in cums]
    qcs = [(q * jnp.exp2(cum)).astype(BF16) for q, cum in zip(qs, cums)]
    krs = [(k * jnp.exp2(tot - cum)).astype(BF16) for k, tot, cum in zip(ks, tots, cums)]
    ovs = [jnp.dot(s.astype(BF16), v, preferred_element_type=F32) for s, v in zip(scores, vs)]
    us = [lax.dot_general(kr, v, (((0,), (0,)), ((), ())), preferred_element_type=F32) for kr, v in zip(krs, vs)]
    dcols = [jnp.broadcast_to(jnp.exp2(tot), (LANES, LANES)).T for tot in tots]
    if dv > LANES:
        dcols = [jnp.concatenate([d] * (dv // LANES), axis=1) for d in dcols]
    return qcs, ovs, us, dcols


def _pad_rows(x, rows):
    if x.shape[0] == rows:
        return x
    return jnp.concatenate([x, jnp.zeros((rows - x.shape[0], x.shape[1]), x.dtype)], axis=0)


def _rec_kernel(q_ref, k_ref, v_ref, g_ref, ahi_ref, alo_ref, s0_ref, nw_ref, mall_ref, lev_ref,
                o_ref, sout_ref, s_scr, *, bb, tb):
    ti = pl.program_id(2)

    @pl.when(ti == 0)
    def _():
        s_scr[...] = s0_ref[...]

    rows = min(tb, CHUNK)
    items = [(b, slice(r0, r0 + rows)) for b in range(bb) for r0 in range(0, tb, rows)]
    qs = [_pad_rows(q_ref[b, sl, :].astype(F32), CHUNK) for b, sl in items]
    ks = [_pad_rows(k_ref[b, sl, :].astype(F32), CHUNK) for b, sl in items]
    vs = [_pad_rows(v_ref[b, sl, :], CHUNK) for b, sl in items]
    as_ = [jnp.concatenate([_pad_rows(ahi_ref[b, sl, :], CHUNK), _pad_rows(alo_ref[b, sl, :], CHUNK)], axis=0)
           for b, sl in items]
    qcs, ovs, us, dcols = _chunks_parallel(qs, ks, as_, vs, mall_ref[...], lev_ref[...])

    states = []
    for i, (b, sl) in enumerate(items):
        s = s_scr[b] if sl.start == 0 else s
        states.append(s)
        s = dcols[i] * s + us[i]
        if sl.stop == tb:
            s_scr[b] = s
    outs = [jnp.dot(qc, s.astype(BF16), preferred_element_type=F32) + ov for qc, s, ov in zip(qcs, states, ovs)]

    nw = nw_ref[...]
    for (b, sl), o in zip(items, outs):
        o = o[:rows]
        ms = jnp.mean(o * o, axis=-1, keepdims=True)
        y = o * lax.rsqrt(ms + EPS) * nw
        o_ref[b, sl, :] = (y * g_ref[b, sl, :].astype(F32)).astype(BF16)

    @pl.when(ti == pl.num_programs(2) - 1)
    def _():
        sout_ref[...] = s_scr[...]


def _rec_blocks(t_len, batch):
    tb = min(t_len, CHUNKS_PER_STEP * CHUNK)
    bb = 1 if t_len > CHUNK else min(batch, CHUNKS_PER_STEP)
    return bb, tb


def _recurrence(p, a, s0, norm_w, mall, lev, *, name, q_col, k_col, v_col, g_col):
    b, t, _ = p.shape
    _, heads, dk, dv = s0.shape
    bb, tb = _rec_blocks(t, b)
    blk = lambda width, col: pl.BlockSpec((bb, tb, width), lambda bi, h, ti: (bi, ti, col // width + h))
    const = lambda shape: pl.BlockSpec(shape, lambda bi, h, ti: (0,) * len(shape))
    st = pl.BlockSpec((bb, None, dk, dv), lambda bi, h, ti: (bi, h, 0, 0))
    return pl.pallas_call(
        functools.partial(_rec_kernel, bb=bb, tb=tb),
        grid=(b // bb, heads, t // tb),
        in_specs=[blk(dk, q_col), blk(dk, k_col), blk(dv, v_col), blk(dv, g_col),
                  blk(dk, 0), blk(dk, heads * dk), st, const((1, dv)), const(mall.shape), const(lev.shape)],
        out_specs=[blk(dv, 0), st],
        out_shape=[jax.ShapeDtypeStruct((b, t, heads * dv), BF16), jax.ShapeDtypeStruct(s0.shape, F32)],
        scratch_shapes=[pltpu.VMEM((bb, dk, dv), F32)],
        compiler_params=pltpu.CompilerParams(
            dimension_semantics=("parallel", "parallel", "arbitrary"), vmem_limit_bytes=VMEM_LIMIT),
        name=name,
    )(p, p, p, p, a, a, s0, norm_w, mall, lev)


def _outproj_kernel(ohg_ref, ogla_ref, wt_ref, wb_ref, x_ref, nw_ref, x1_ref, h2_ref):
    m = (jnp.dot(ohg_ref[...], wt_ref[...], preferred_element_type=F32)
         + jnp.dot(ogla_ref[...], wb_ref[...], preferred_element_type=F32))
    x1 = x_ref[...] + m
    x1_ref[...] = x1
    h2_ref[...] = _rms(x1, nw_ref[...]).astype(BF16)


def _outproj(o_hg, o_gla, w_top, w_bot, x, norm_w, tm):
    t, d = x.shape
    half = o_hg.shape[1]
    return pl.pallas_call(
        _outproj_kernel,
        grid=(t // tm,),
        in_specs=[
            pl.BlockSpec((tm, half), lambda i: (i, 0)),
            pl.BlockSpec((tm, half), lambda i: (i, 0)),
            pl.BlockSpec((half, d), lambda i: (0, 0)),
            pl.BlockSpec((half, d), lambda i: (0, 0)),
            pl.BlockSpec((tm, d), lambda i: (i, 0)),
            pl.BlockSpec((1, d), lambda i: (0, 0)),
        ],
        out_specs=[pl.BlockSpec((tm, d), lambda i: (i, 0)), pl.BlockSpec((tm, d), lambda i: (i, 0))],
        out_shape=[jax.ShapeDtypeStruct((t, d), F32), jax.ShapeDtypeStruct((t, d), BF16)],
        compiler_params=pltpu.CompilerParams(
            dimension_semantics=("parallel",), vmem_limit_bytes=VMEM_LIMIT),
        name="outproj",
    )(o_hg, o_gla, w_top, w_bot, x, norm_w)


def _ffn_kernel(h_ref, wg_ref, wu_ref, wd_ref, x1_ref, nw_ref, o_ref):
    j = pl.program_id(1)

    @pl.when(j == 0)
    def _():
        o_ref[...] = x1_ref[...]

    h = h_ref[...]
    g = jnp.dot(h, wg_ref[...], preferred_element_type=F32)
    u = jnp.dot(h, wu_ref[...], preferred_element_type=F32)
    mid = ((g * _sigmoid(g)) * u).astype(BF16)
    o_ref[...] += jnp.dot(mid, wd_ref[...], preferred_element_type=F32)

    @pl.when(j == pl.num_programs(1) - 1)
    def _():
        o_ref[...] = _rms(o_ref[...], nw_ref[...])


def _ffn(h2, w_gate_up, w_down, x1, norm_w, tm, tf):
    t, d = x1.shape
    nf = D_FF // tf
    return pl.pallas_call(
        _ffn_kernel,
        grid=(t // tm, nf),
        in_specs=[
            pl.BlockSpec((tm, d), lambda i, j: (i, 0)),
            pl.BlockSpec((d, tf), lambda i, j: (0, j)),
            pl.BlockSpec((d, tf), lambda i, j: (0, nf + j)),
            pl.BlockSpec((tf, d), lambda i, j: (j, 0)),
            pl.BlockSpec((tm, d), lambda i, j: (i, 0)),
            pl.BlockSpec((1, d), lambda i, j: (0, 0)),
        ],
        out_specs=pl.BlockSpec((tm, d), lambda i, j: (i, 0)),
        out_shape=jax.ShapeDtypeStruct((t, d), F32),
        compiler_params=pltpu.CompilerParams(
            dimension_semantics=("parallel", "arbitrary"), vmem_limit_bytes=VMEM_LIMIT),
        name="ffn",
    )(h2, w_gate_up, w_gate_up, w_down, x1, norm_w)


def _trunk(x, s_hg, s_gla, wts, consts):
    b, t, d = x.shape
    xf = x.reshape(b * t, d)
    mall, lev = consts
    p, a_hg, a_gla = _inproj(xf, wts["norm_mix"], wts["w_in"], wts["w_gr"], wts["lb_logits"], wts["w_gk2"],
                             wts["b_gk"], 512)
    p = p.reshape(b, t, -1)
    o_hg, s_hg_new = _recurrence(p, a_hg.reshape(b, t, -1), s_hg, wts["hg_norm"], mall, lev, name="hgrn",
                                 q_col=0, k_col=IN_TILE, v_col=2 * IN_TILE, g_col=3 * IN_TILE)
    gla_q = 4 * IN_TILE
    o_gla, s_gla_new = _recurrence(p, a_gla.reshape(b, t, -1), s_gla, wts["gla_norm"], mall, lev, name="gla",
                                   q_col=gla_q, k_col=gla_q + GLA_HEADS * GLA_DK, v_col=5 * IN_TILE,
                                   g_col=6 * IN_TILE)
    x1, h2 = _outproj(o_hg.reshape(b * t, -1), o_gla.reshape(b * t, -1), wts["w_out_top"], wts["w_out_bot"],
                      xf, wts["norm_ffn"], 512)
    y = _ffn(h2, wts["w_gate_up"], wts["w_down"], x1, wts["norm_final"], 512, 512)
    return y.reshape(b, t, d), s_hg_new[None], s_gla_new[None]


def kernel(x_prompt, x_sample, state_hgrn, state_gla, lb_logits, norm_mix, w_in, w_gk2, b_gk, hg_norm, gla_norm,
           w_out, norm_ffn, w_gate_up, w_down, norm_final):
    w = w_in[0].astype(BF16)
    hw = HG_HEADS * HG_DK
    w_gr = jnp.pad(w[:, P_COLS:P_COLS + GLA_RANK], ((0, 0), (0, LANES - GLA_RANK)))
    w2 = jnp.pad(w_gk2[0], ((0, LANES - GLA_RANK), (0, 0))).astype(BF16)
    wo = w_out[0].astype(BF16)
    wts = {
        "norm_mix": norm_mix[0][None, :], "w_in": w, "w_gr": w_gr,
        "lb_logits": lb_logits, "hg_norm": hg_norm[0][None, :],
        "w_gk2": w2, "b_gk": b_gk[0][None, :], "gla_norm": gla_norm[0][None, :],
        "w_out_top": wo[:hw], "w_out_bot": wo[hw:], "norm_ffn": norm_ffn[0][None, :],
        "w_gate_up": w_gate_up[0].astype(BF16), "w_down": w_down[0].astype(BF16),
        "norm_final": norm_final[None, :],
    }
    mall_np, lev_np = _chunk_constants()
    consts = (jnp.asarray(mall_np, BF16), jnp.asarray(lev_np))

    bp = x_prompt.shape[0]
    zero_hg = jnp.zeros((bp, HG_HEADS, HG_DK, HG_DV), F32)
    zero_gla = jnp.zeros((bp, GLA_HEADS, GLA_DK, GLA_DV), F32)
    y_p, hg_p, gla_p = _trunk(x_prompt, zero_hg, zero_gla, wts, consts)
    y_s, hg_s, gla_s = _trunk(x_sample, state_hgrn[0], state_gla[0], wts, consts)
    return (y_p, y_s, hg_p, gla_p, hg_s, gla_s)
```

```python
import functools

import numpy as np
import jax
import jax.numpy as jnp
from jax import lax
from jax.experimental import pallas as pl
from jax.experimental.pallas import tpu as pltpu

F32 = jnp.float32
BF16 = jnp.bfloat16

D_MODEL = 2048
HG_HEADS = 8
HG_DK = 128
HG_DV = 128
GLA_HEADS = 4
GLA_DK = 128
GLA_DV = 256
GLA_RANK = 16
GLA_GATE_NORMALIZER = 16.0
D_FF = 5632
EPS = 1e-6
LOG2E = 1.4426950408889634

LANES = 128
CHUNK = 128
LEVELS = 7
MXU_LEVELS = 3
CHUNKS_PER_STEP = 16
VMEM_LIMIT = 56 * 1024 * 1024
TM_INPROJ = 1024
TM_OUTPROJ = 512
TM_FFN = 1024
TF_FFN = 256

IN_TILE = 1024
IN_TILES = 7
IN_SUB = 256
P_COLS = IN_TILE * IN_TILES


def _rms(x, w):
    ms = jnp.mean(x * x, axis=-1, keepdims=True)
    return (x * lax.rsqrt(ms + EPS)) * w


def _sigmoid(x):
    return 1.0 / (1.0 + jnp.exp(-x))


def _silu(x):
    return x * _sigmoid(x)


def _inproj_kernel(x_ref, nw_ref, w_ref, wr_ref, lbl_ref, w2_ref, b2_ref, p_ref, ahg_ref, agla_ref, h_scr):
    j = pl.program_id(1)
    hg = HG_HEADS * HG_DK
    gl = GLA_HEADS * GLA_DK

    @pl.when(j == 0)
    def _():
        h_scr[...] = _rms(x_ref[...], nw_ref[...]).astype(BF16)

    def forget_gate(z, cols):
        l0 = lbl_ref[0:1, cols]
        l1 = lbl_ref[1:2, cols]
        mx = jnp.maximum(l0, l1)
        e0 = jnp.exp(l0 - mx)
        e1 = jnp.exp(l1 - mx)
        lb = e0 / (e0 + e1)
        ez = jnp.exp(-jnp.abs(z))
        big = 1.0 / (1.0 + ez)
        small = ez * big
        pos = z >= 0.0
        a = jnp.log(lb + (1.0 - lb) * jnp.where(pos, big, small)) * LOG2E
        hi = a.astype(BF16)
        ahg_ref[:, cols] = hi
        ahg_ref[:, hg + cols.start:hg + cols.stop] = (a - hi.astype(F32)).astype(BF16)
        return (1.0 - lb) * jnp.where(pos, small, big)

    activations = (
        lambda p, cols: _silu(p) * (HG_DK ** -0.5),
        forget_gate,
        lambda p, cols: p,
        lambda p, cols: _silu(p),
        lambda p, cols: p * ((GLA_DK ** -0.5) if cols.stop <= gl else 1.0),
        lambda p, cols: p,
        lambda p, cols: _silu(p),
    )

    def low_rank_gate():
        gr = jnp.dot(h_scr[...], wr_ref[...], preferred_element_type=F32).astype(BF16)
        x2 = (jnp.dot(gr, w2_ref[...], preferred_element_type=F32) + b2_ref[...]) * LOG2E
        a = (jnp.minimum(x2, 0.0) - jnp.log(1.0 + jnp.exp2(-jnp.abs(x2))) * LOG2E) * (1.0 / GLA_GATE_NORMALIZER)
        hi = a.astype(BF16)
        agla_ref[:, :gl] = hi
        agla_ref[:, gl:] = (a - hi.astype(F32)).astype(BF16)

    for tile in range(IN_TILES):
        @pl.when(j == tile)
        def _(tile=tile):
            for c0 in range(0, IN_TILE, IN_SUB):
                cols = slice(c0, c0 + IN_SUB)
                p = jnp.dot(h_scr[...], w_ref[:, cols], preferred_element_type=F32)
                p_ref[:, cols] = activations[tile](p, cols).astype(BF16)
            if tile == IN_TILES - 1:
                low_rank_gate()


def _inproj(x, norm_w, w, w_gr, lb_logits, w2, b2, tm):
    t, d = x.shape
    hg = HG_HEADS * HG_DK
    gl = GLA_HEADS * GLA_DK
    const = lambda shape: pl.BlockSpec(shape, lambda i, j: (0,) * len(shape))
    return pl.pallas_call(
        _inproj_kernel,
        grid=(t // tm, IN_TILES),
        in_specs=[
            pl.BlockSpec((tm, d), lambda i, j: (i, 0)),
            const((1, d)),
            pl.BlockSpec((d, IN_TILE), lambda i, j: (0, j)),
            const((d, LANES)), const((2, hg)), const((LANES, gl)), const((1, gl)),
        ],
        out_specs=[pl.BlockSpec((tm, IN_TILE), lambda i, j: (i, j)),
                   pl.BlockSpec((tm, 2 * hg), lambda i, j: (i, 0)),
                   pl.BlockSpec((tm, 2 * gl), lambda i, j: (i, 0))],
        out_shape=[jax.ShapeDtypeStruct((t, P_COLS), BF16),
                   jax.ShapeDtypeStruct((t, 2 * hg), BF16),
                   jax.ShapeDtypeStruct((t, 2 * gl), BF16)],
        scratch_shapes=[pltpu.VMEM((tm, d), BF16)],
        compiler_params=pltpu.CompilerParams(
            dimension_semantics=("parallel", "arbitrary"), vmem_limit_bytes=VMEM_LIMIT),
        name="inproj",
    )(x, norm_w, w, w_gr, lb_logits, w2, b2)


def _chunk_constants():
    c = CHUNK
    t = np.arange(c)[:, None]
    u = np.arange(c)[None, :]
    mats = []
    for l in range(MXU_LEVELS):
        h = 1 << l
        mid = (t // (2 * h)) * (2 * h) + h - 1
        upper = (t & h) != 0
        mats.append(np.where(upper, (u > mid) & (u <= t), (u > t) & (u <= mid)))
    mats.append(u <= t)
    mall = np.concatenate(mats, axis=0).astype(np.float32)
    mall = np.concatenate([mall, mall], axis=1)
    x = t ^ u
    lev = np.where(t > u, np.floor(np.log2(np.maximum(x, 1))).astype(np.int32), -1)
    lev = np.where(t == u, LEVELS, lev).astype(np.int32)
    return mall, lev


def _aligned_operands(q, k, cum, l):
    h = 1 << l
    lows, ups = [], []
    for base in range(0, CHUNK, 2 * h):
        mid = cum[base + h - 1:base + h, :]
        lows.append(k[base:base + h] * jnp.exp2(mid - cum[base:base + h]))
        ups.append(q[base + h:base + 2 * h] * jnp.exp2(cum[base + h:base + 2 * h] - mid))
    x_up = jnp.concatenate(ups, axis=0).astype(BF16)
    x = jnp.concatenate([z for pair in zip(lows, ups) for z in pair], axis=0).astype(BF16)
    return x_up, x


def _aligned_update(scores, sc, member, l):
    h = 1 << l
    parts = []
    for i, base in enumerate(range(0, CHUNK, 2 * h)):
        rows = slice(base + h, base + 2 * h)
        parts.append(scores[base:base + h])
        parts.append(jnp.where(member[rows], sc[i * h:(i + 1) * h], scores[rows]))
    return jnp.concatenate(parts, axis=0)


def _chunks_parallel(qs, ks, as_, vs, mall, lev):
    c = CHUNK
    n = len(qs)
    dv = vs[0].shape[1]
    e_all = jnp.dot(mall, jnp.concatenate(as_, axis=1), preferred_element_type=F32)
    es = [e_all[:, i * LANES:(i + 1) * LANES] for i in range(n)]
    cums = [e[MXU_LEVELS * c:] for e in es]

    row = lax.broadcasted_iota(jnp.int32, (c, LANES), 0)
    scores = [jnp.where(lev == LEVELS, jnp.sum(q * k, axis=1, keepdims=True), 0.0) for q, k in zip(qs, ks)]
    for l in range(MXU_LEVELS):
        upper = (row & (1 << l)) != 0
        member = lev == l
        xs = [(jnp.where(upper, q, k) * jnp.exp2(e[l * c:(l + 1) * c])).astype(BF16) for q, k, e in zip(qs, ks, es)]
        scs = [lax.dot_general(x, x, (((1,), (1,)), ((), ())), preferred_element_type=F32) for x in xs]
        scores = [jnp.where(member, sc, s) for sc, s in zip(scs, scores)]
    for l in range(MXU_LEVELS, LEVELS):
        member = lev == l
        ops = [_aligned_operands(q, k, cum, l) for q, k, cum in zip(qs, ks, cums)]
        scs = [lax.dot_general(x_up, x, (((1,), (1,)), ((), ())), preferred_element_type=F32) for x_up, x in ops]
        scores = [_aligned_update(s, sc, member, l) for s, sc in zip(scores, scs)]

    tots = [cum[c - 1:c, :] for cum in cums]
    qcs = [(q * jnp.exp2(cum)).astype(BF16) for q, cum in zip(qs, cums)]
    krs = [(k * jnp.exp2(tot - cum)).astype(BF16) for k, tot, cum in zip(ks, tots, cums)]
    ovs = [jnp.dot(s.astype(BF16), v, preferred_element_type=F32) for s, v in zip(scores, vs)]
    us = [lax.dot_general(kr, v, (((0,), (0,)), ((), ())), preferred_element_type=F32) for kr, v in zip(krs, vs)]
    dcols = [jnp.broadcast_to(jnp.exp2(tot), (LANES, LANES)).T for tot in tots]
    if dv > LANES:
        dcols = [jnp.concatenate([d] * (dv // LANES), axis=1) for d in dcols]
    return qcs, ovs, us, dcols


def _pad_rows(x, rows):
    if x.shape[0] == rows:
        return x
    return jnp.concatenate([x, jnp.zeros((rows - x.shape[0], x.shape[1]), x.dtype)], axis=0)


def _rec_kernel(q_ref, k_ref, v_ref, g_ref, ahi_ref, alo_ref, s0_ref, nw_ref, mall_ref, lev_ref,
                o_ref, sout_ref, s_scr, *, bb, tb):
    ti = pl.program_id(2)

    @pl.when(ti == 0)
    def _():
        s_scr[...] = s0_ref[...]

    rows = min(tb, CHUNK)
    items = [(b, slice(r0, r0 + rows)) for b in range(bb) for r0 in range(0, tb, rows)]
    qs = [_pad_rows(q_ref[b, sl, :].astype(F32), CHUNK) for b, sl in items]
    ks = [_pad_rows(k_ref[b, sl, :].astype(F32), CHUNK) for b, sl in items]
    vs = [_pad_rows(v_ref[b, sl, :], CHUNK) for b, sl in items]
    as_ = [jnp.concatenate([_pad_rows(ahi_ref[b, sl, :], CHUNK), _pad_rows(alo_ref[b, sl, :], CHUNK)], axis=0)
           for b, sl in items]
    qcs, ovs, us, dcols = _chunks_parallel(qs, ks, as_, vs, mall_ref[...], lev_ref[...])

    states = []
    for i, (b, sl) in enumerate(items):
        s = s_scr[b] if sl.start == 0 else s
        states.append(s)
        s = dcols[i] * s + us[i]
        if sl.stop == tb:
            s_scr[b] = s
    outs = [jnp.dot(qc, s.astype(BF16), preferred_element_type=F32) + ov for qc, s, ov in zip(qcs, states, ovs)]

    nw = nw_ref[...]
    for (b, sl), o in zip(items, outs):
        o = o[:rows]
        ms = jnp.mean(o * o, axis=-1, keepdims=True)
        y = o * lax.rsqrt(ms + EPS) * nw
        o_ref[b, sl, :] = (y * g_ref[b, sl, :].astype(F32)).astype(BF16)

    @pl.when(ti == pl.num_programs(2) - 1)
    def _():
        sout_ref[...] = s_scr[...]


def _rec_blocks(t_len, batch):
    tb = min(t_len, CHUNKS_PER_STEP * CHUNK)
    bb = 1 if t_len > CHUNK else min(batch, CHUNKS_PER_STEP)
    return bb, tb


def _recurrence(p, a, s0, norm_w, mall, lev, *, name, q_col, k_col, v_col, g_col):
    b, t, _ = p.shape
    _, heads, dk, dv = s0.shape
    bb, tb = _rec_blocks(t, b)
    blk = lambda width, col: pl.BlockSpec((bb, tb, width), lambda bi, h, ti: (bi, ti, col // width + h))
    const = lambda shape: pl.BlockSpec(shape, lambda bi, h, ti: (0,) * len(shape))
    st = pl.BlockSpec((bb, None, dk, dv), lambda bi, h, ti: (bi, h, 0, 0))
    return pl.pallas_call(
        functools.partial(_rec_kernel, bb=bb, tb=tb),
        grid=(b // bb, heads, t // tb),
        in_specs=[blk(dk, q_col), blk(dk, k_col), blk(dv, v_col), blk(dv, g_col),
                  blk(dk, 0), blk(dk, heads * dk), st, const((1, dv)), const(mall.shape), const(lev.shape)],
        out_specs=[blk(dv, 0), st],
        out_shape=[jax.ShapeDtypeStruct((b, t, heads * dv), BF16), jax.ShapeDtypeStruct(s0.shape, F32)],
        scratch_shapes=[pltpu.VMEM((bb, dk, dv), F32)],
        compiler_params=pltpu.CompilerParams(
            dimension_semantics=("parallel", "parallel", "arbitrary"), vmem_limit_bytes=VMEM_LIMIT),
        name=name,
    )(p, p, p, p, a, a, s0, norm_w, mall, lev)


def _outproj_kernel(ohg_ref, ogla_ref, wt_ref, wb_ref, x_ref, nw_ref, x1_ref, h2_ref):
    m = (jnp.dot(ohg_ref[...], wt_ref[...], preferred_element_type=F32)
         + jnp.dot(ogla_ref[...], wb_ref[...], preferred_element_type=F32))
    x1 = x_ref[...] + m
    x1_ref[...] = x1
    h2_ref[...] = _rms(x1, nw_ref[...]).astype(BF16)


def _outproj(o_hg, o_gla, w_top, w_bot, x, norm_w, tm):
    t, d = x.shape
    half = o_hg.shape[1]
    return pl.pallas_call(
        _outproj_kernel,
        grid=(t // tm,),
        in_specs=[
            pl.BlockSpec((tm, half), lambda i: (i, 0)),
            pl.BlockSpec((tm, half), lambda i: (i, 0)),
            pl.BlockSpec((half, d), lambda i: (0, 0)),
            pl.BlockSpec((half, d), lambda i: (0, 0)),
            pl.BlockSpec((tm, d), lambda i: (i, 0)),
            pl.BlockSpec((1, d), lambda i: (0, 0)),
        ],
        out_specs=[pl.BlockSpec((tm, d), lambda i: (i, 0)), pl.BlockSpec((tm, d), lambda i: (i, 0))],
        out_shape=[jax.ShapeDtypeStruct((t, d), F32), jax.ShapeDtypeStruct((t, d), BF16)],
        compiler_params=pltpu.CompilerParams(
            dimension_semantics=("parallel",), vmem_limit_bytes=VMEM_LIMIT),
        name="outproj",
    )(o_hg, o_gla, w_top, w_bot, x, norm_w)


def _ffn_kernel(h_ref, wg_ref, wu_ref, wd_ref, x1_ref, nw_ref, o_ref):
    j = pl.program_id(1)

    @pl.when(j == 0)
    def _():
        o_ref[...] = x1_ref[...]

    h = h_ref[...]
    g = jnp.dot(h, wg_ref[...], preferred_element_type=F32)
    u = jnp.dot(h, wu_ref[...], preferred_element_type=F32)
    mid = ((g * _sigmoid(g)) * u).astype(BF16)
    o_ref[...] += jnp.dot(mid, wd_ref[...], preferred_element_type=F32)

    @pl.when(j == pl.num_programs(1) - 1)
    def _():
        o_ref[...] = _rms(o_ref[...], nw_ref[...])


def _ffn(h2, w_gate_up, w_down, x1, norm_w, tm, tf):
    t, d = x1.shape
    nf = D_FF // tf
    return pl.pallas_call(
        _ffn_kernel,
        grid=(t // tm, nf),
        in_specs=[
            pl.BlockSpec((tm, d), lambda i, j: (i, 0)),
            pl.BlockSpec((d, tf), lambda i, j: (0, j)),
            pl.BlockSpec((d, tf), lambda i, j: (0, nf + j)),
            pl.BlockSpec((tf, d), lambda i, j: (j, 0)),
            pl.BlockSpec((tm, d), lambda i, j: (i, 0)),
            pl.BlockSpec((1, d), lambda i, j: (0, 0)),
        ],
        out_specs=pl.BlockSpec((tm, d), lambda i, j: (i, 0)),
        out_shape=jax.ShapeDtypeStruct((t, d), F32),
        compiler_params=pltpu.CompilerParams(
            dimension_semantics=("parallel", "arbitrary"), vmem_limit_bytes=VMEM_LIMIT),
        name="ffn",
    )(h2, w_gate_up, w_gate_up, w_down, x1, norm_w)


def _trunk(x, s_hg, s_gla, wts, consts):
    b, t, d = x.shape
    xf = x.reshape(b * t, d)
    mall, lev = consts
    p, a_hg, a_gla = _inproj(xf, wts["norm_mix"], wts["w_in"], wts["w_gr"], wts["lb_logits"], wts["w_gk2"],
                             wts["b_gk"], TM_INPROJ)
    p = p.reshape(b, t, -1)
    o_hg, s_hg_new = _recurrence(p, a_hg.reshape(b, t, -1), s_hg, wts["hg_norm"], mall, lev, name="hgrn",
                                 q_col=0, k_col=IN_TILE, v_col=2 * IN_TILE, g_col=3 * IN_TILE)
    gla_q = 4 * IN_TILE
    o_gla, s_gla_new = _recurrence(p, a_gla.reshape(b, t, -1), s_gla, wts["gla_norm"], mall, lev, name="gla",
                                   q_col=gla_q, k_col=gla_q + GLA_HEADS * GLA_DK, v_col=5 * IN_TILE,
                                   g_col=6 * IN_TILE)
    x1, h2 = _outproj(o_hg.reshape(b * t, -1), o_gla.reshape(b * t, -1), wts["w_out_top"], wts["w_out_bot"],
                      xf, wts["norm_ffn"], TM_OUTPROJ)
    y = _ffn(h2, wts["w_gate_up"], wts["w_down"], x1, wts["norm_final"], TM_FFN, TF_FFN)
    return y.reshape(b, t, d), s_hg_new[None], s_gla_new[None]


def kernel(x_prompt, x_sample, state_hgrn, state_gla, lb_logits, norm_mix, w_in, w_gk2, b_gk, hg_norm, gla_norm,
           w_out, norm_ffn, w_gate_up, w_down, norm_final):
    w = w_in[0].astype(BF16)
    hw = HG_HEADS * HG_DK
    w_gr = jnp.pad(w[:, P_COLS:P_COLS + GLA_RANK], ((0, 0), (0, LANES - GLA_RANK)))
    w2 = jnp.pad(w_gk2[0], ((0, LANES - GLA_RANK), (0, 0))).astype(BF16)
    wo = w_out[0].astype(BF16)
    wts = {
        "norm_mix": norm_mix[0][None, :], "w_in": w, "w_gr": w_gr,
        "lb_logits": lb_logits, "hg_norm": hg_norm[0][None, :],
        "w_gk2": w2, "b_gk": b_gk[0][None, :], "gla_norm": gla_norm[0][None, :],
        "w_out_top": wo[:hw], "w_out_bot": wo[hw:], "norm_ffn": norm_ffn[0][None, :],
        "w_gate_up": w_gate_up[0].astype(BF16), "w_down": w_down[0].astype(BF16),
        "norm_final": norm_final[None, :],
    }
    mall_np, lev_np = _chunk_constants()
    consts = (jnp.asarray(mall_np, BF16), jnp.asarray(lev_np))

    bp = x_prompt.shape[0]
    zero_hg = jnp.zeros((bp, HG_HEADS, HG_DK, HG_DV), F32)
    zero_gla = jnp.zeros((bp, GLA_HEADS, GLA_DK, GLA_DV), F32)
    y_p, hg_p, gla_p = _trunk(x_prompt, zero_hg, zero_gla, wts, consts)
    y_s, hg_s, gla_s = _trunk(x_sample, state_hgrn[0], state_gla[0], wts, consts)
    return (y_p, y_s, hg_p, gla_p, hg_s, gla_s)
```

```python
import functools

import numpy as np
import jax
import jax.numpy as jnp
from jax import lax
from jax.experimental import pallas as pl
from jax.experimental.pallas import tpu as pltpu

F32 = jnp.float32
BF16 = jnp.bfloat16

D_MODEL = 2048
HG_HEADS = 8
HG_DK = 128
HG_DV = 128
GLA_HEADS = 4
GLA_DK = 128
GLA_DV = 256
GLA_RANK = 16
GLA_GATE_NORMALIZER = 16.0
D_FF = 5632
EPS = 1e-6
LOG2E = 1.4426950408889634

LANES = 128
CHUNK = 128
LEVELS = 7
MXU_LEVELS = 3
CHUNKS_PER_STEP = 16
VMEM_LIMIT = 56 * 1024 * 1024
TM_INPROJ = 1024
TM_OUTPROJ = 512
TM_FFN = 1024
TF_FFN = 256

IN_TILE = 1024
IN_TILES = 7
IN_SUB = 256
P_COLS = IN_TILE * IN_TILES


def _rms(x, w):
    ms = jnp.mean(x * x, axis=-1, keepdims=True)
    return (x * lax.rsqrt(ms + EPS)) * w


def _sigmoid(x):
    return 1.0 / (1.0 + jnp.exp(-x))


def _silu(x):
    return x * _sigmoid(x)


def _inproj_kernel(x_ref, nw_ref, w_ref, wr_ref, lbl_ref, w2_ref, b2_ref, p_ref, ahg_ref, agla_ref, h_scr):
    j = pl.program_id(1)
    hg = HG_HEADS * HG_DK
    gl = GLA_HEADS * GLA_DK

    @pl.when(j == 0)
    def _():
        h_scr[...] = _rms(x_ref[...], nw_ref[...]).astype(BF16)

    def forget_gate(z, cols):
        l0 = lbl_ref[0:1, cols]
        l1 = lbl_ref[1:2, cols]
        mx = jnp.maximum(l0, l1)
        e0 = jnp.exp(l0 - mx)
        e1 = jnp.exp(l1 - mx)
        lb = e0 / (e0 + e1)
        ez = jnp.exp(-jnp.abs(z))
        big = 1.0 / (1.0 + ez)
        small = ez * big
        pos = z >= 0.0
        a = jnp.log(lb + (1.0 - lb) * jnp.where(pos, big, small)) * LOG2E
        hi = a.astype(BF16)
        ahg_ref[:, cols] = hi
        ahg_ref[:, hg + cols.start:hg + cols.stop] = (a - hi.astype(F32)).astype(BF16)
        return (1.0 - lb) * jnp.where(pos, small, big)

    activations = (
        lambda p, cols: _silu(p) * (HG_DK ** -0.5),
        forget_gate,
        lambda p, cols: p,
        lambda p, cols: _silu(p),
        lambda p, cols: p * ((GLA_DK ** -0.5) if cols.stop <= gl else 1.0),
        lambda p, cols: p,
        lambda p, cols: _silu(p),
    )

    def low_rank_gate():
        gr = jnp.dot(h_scr[...], wr_ref[...], preferred_element_type=F32).astype(BF16)
        x2 = (jnp.dot(gr, w2_ref[...], preferred_element_type=F32) + b2_ref[...]) * LOG2E
        a = (jnp.minimum(x2, 0.0) - jnp.log(1.0 + jnp.exp2(-jnp.abs(x2))) * LOG2E) * (1.0 / GLA_GATE_NORMALIZER)
        hi = a.astype(BF16)
        agla_ref[:, :gl] = hi
        agla_ref[:, gl:] = (a - hi.astype(F32)).astype(BF16)

    for tile in range(IN_TILES):
        @pl.when(j == tile)
        def _(tile=tile):
            for c0 in range(0, IN_TILE, IN_SUB):
                cols = slice(c0, c0 + IN_SUB)
                p = jnp.dot(h_scr[...], w_ref[:, cols], preferred_element_type=F32)
                p_ref[:, cols] = activations[tile](p, cols).astype(BF16)
            if tile == IN_TILES - 1:
                low_rank_gate()


def _inproj(x, norm_w, w, w_gr, lb_logits, w2, b2, tm):
    t, d = x.shape
    hg = HG_HEADS * HG_DK
    gl = GLA_HEADS * GLA_DK
    const = lambda shape: pl.BlockSpec(shape, lambda i, j: (0,) * len(shape))
    return pl.pallas_call(
        _inproj_kernel,
        grid=(t // tm, IN_TILES),
        in_specs=[
            pl.BlockSpec((tm, d), lambda i, j: (i, 0)),
            const((1, d)),
            pl.BlockSpec((d, IN_TILE), lambda i, j: (0, j)),
            const((d, LANES)), const((2, hg)), const((LANES, gl)), const((1, gl)),
        ],
        out_specs=[pl.BlockSpec((tm, IN_TILE), lambda i, j: (i, j)),
                   pl.BlockSpec((tm, 2 * hg), lambda i, j: (i, 0)),
                   pl.BlockSpec((tm, 2 * gl), lambda i, j: (i, 0))],
        out_shape=[jax.ShapeDtypeStruct((t, P_COLS), BF16),
                   jax.ShapeDtypeStruct((t, 2 * hg), BF16),
                   jax.ShapeDtypeStruct((t, 2 * gl), BF16)],
        scratch_shapes=[pltpu.VMEM((tm, d), BF16)],
        compiler_params=pltpu.CompilerParams(
            dimension_semantics=("parallel", "arbitrary"), vmem_limit_bytes=VMEM_LIMIT),
        name="inproj",
    )(x, norm_w, w, w_gr, lb_logits, w2, b2)


def _chunk_constants():
    c = CHUNK
    t = np.arange(c)[:, None]
    u = np.arange(c)[None, :]
    mats = []
    for l in range(MXU_LEVELS):
        h = 1 << l
        mid = (t // (2 * h)) * (2 * h) + h - 1
        upper = (t & h) != 0
        mats.append(np.where(upper, (u > mid) & (u <= t), (u > t) & (u <= mid)))
    mats.append(u <= t)
    mall = np.concatenate(mats, axis=0).astype(np.float32)
    mall = np.concatenate([mall, mall], axis=1)
    x = t ^ u
    lev = np.where(t > u, np.floor(np.log2(np.maximum(x, 1))).astype(np.int32), -1)
    lev = np.where(t == u, LEVELS, lev).astype(np.int32)
    return mall, lev


def _aligned_operands(q, k, cum, l):
    h = 1 << l
    lows, ups = [], []
    for base in range(0, CHUNK, 2 * h):
        mid = cum[base + h - 1:base + h, :]
        lows.append(k[base:base + h] * jnp.exp2(mid - cum[base:base + h]))
        ups.append(q[base + h:base + 2 * h] * jnp.exp2(cum[base + h:base + 2 * h] - mid))
    x_up = jnp.concatenate(ups, axis=0).astype(BF16)
    x = jnp.concatenate([z for pair in zip(lows, ups) for z in pair], axis=0)
    return x_up, x.T.astype(BF16)


def _aligned_update(scores, sc, member, l):
    h = 1 << l
    parts = []
    for i, base in enumerate(range(0, CHUNK, 2 * h)):
        rows = slice(base + h, base + 2 * h)
        parts.append(scores[base:base + h])
        parts.append(jnp.where(member[rows], sc[i * h:(i + 1) * h], scores[rows]))
    return jnp.concatenate(parts, axis=0)


def _chunks_parallel(qs, ks, as_, vs, mall, lev):
    c = CHUNK
    n = len(qs)
    dv = vs[0].shape[1]
    e_all = jnp.dot(mall, jnp.concatenate(as_, axis=1), preferred_element_type=F32)
    es = [e_all[:, i * LANES:(i + 1) * LANES] for i in range(n)]
    cums = [e[MXU_LEVELS * c:] for e in es]

    row = lax.broadcasted_iota(jnp.int32, (c, LANES), 0)
    uppers = [(row & (1 << l)) != 0 for l in range(MXU_LEVELS)]
    tots = [cum[c - 1:c, :] for cum in cums]
    operands, qcs, krs = [], [], []
    for q, k, e, cum, tot in zip(qs, ks, es, cums, tots):
        ops = []
        for l in range(MXU_LEVELS):
            x = jnp.where(uppers[l], q, k) * jnp.exp2(e[l * c:(l + 1) * c])
            ops.append((x.astype(BF16), x.T.astype(BF16)))
        for l in range(MXU_LEVELS, LEVELS):
            ops.append(_aligned_operands(q, k, cum, l))
        operands.append(ops)
        qcs.append((q * jnp.exp2(cum)).astype(BF16))
        krs.append((k * jnp.exp2(tot - cum)).astype(BF16))
    products = [[jnp.dot(x_up, x_t, preferred_element_type=F32) for x_up, x_t in ops]
                for ops in operands]
    members = [lev == l for l in range(LEVELS)]
    scores = []
    for q, k, scs in zip(qs, ks, products):
        s = jnp.where(lev == LEVELS, jnp.sum(q * k, axis=1, keepdims=True), 0.0)
        for l in range(MXU_LEVELS):
            s = jnp.where(members[l], scs[l], s)
        for l in range(MXU_LEVELS, LEVELS):
            s = _aligned_update(s, scs[l], members[l], l)
        scores.append(s.astype(BF16))

    ovs = [jnp.dot(s, v, preferred_element_type=F32) for s, v in zip(scores, vs)]
    us = [lax.dot_general(kr, v, (((0,), (0,)), ((), ())), preferred_element_type=F32) for kr, v in zip(krs, vs)]
    dcols = [jnp.broadcast_to(jnp.exp2(tot), (LANES, LANES)).T for tot in tots]
    if dv > LANES:
        dcols = [jnp.concatenate([d] * (dv // LANES), axis=1) for d in dcols]
    return qcs, ovs, us, dcols


def _pad_rows(x, rows):
    if x.shape[0] == rows:
        return x
    return jnp.concatenate([x, jnp.zeros((rows - x.shape[0], x.shape[1]), x.dtype)], axis=0)


def _rec_kernel(q_ref, k_ref, v_ref, g_ref, ahi_ref, alo_ref, s0_ref, nw_ref, mall_ref, lev_ref,
                o_ref, sout_ref, s_scr, *, bb, tb):
    ti = pl.program_id(2)

    @pl.when(ti == 0)
    def _():
        s_scr[...] = s0_ref[...]

    rows = min(tb, CHUNK)
    items = [(b, slice(r0, r0 + rows)) for b in range(bb) for r0 in range(0, tb, rows)]
    qs = [_pad_rows(q_ref[b, sl, :].astype(F32), CHUNK) for b, sl in items]
    ks = [_pad_rows(k_ref[b, sl, :].astype(F32), CHUNK) for b, sl in items]
    vs = [_pad_rows(v_ref[b, sl, :], CHUNK) for b, sl in items]
    as_ = [jnp.concatenate([_pad_rows(ahi_ref[b, sl, :], CHUNK), _pad_rows(alo_ref[b, sl, :], CHUNK)], axis=0)
           for b, sl in items]
    qcs, ovs, us, dcols = _chunks_parallel(qs, ks, as_, vs, mall_ref[...], lev_ref[...])

    states = []
    for i, (b, sl) in enumerate(items):
        s = s_scr[b] if sl.start == 0 else s
        states.append(s)
        s = dcols[i] * s + us[i]
        if sl.stop == tb:
            s_scr[b] = s
    outs = [jnp.dot(qc, s.astype(BF16), preferred_element_type=F32) + ov for qc, s, ov in zip(qcs, states, ovs)]

    nw = nw_ref[...]
    for (b, sl), o in zip(items, outs):
        o = o[:rows]
        ms = jnp.mean(o * o, axis=-1, keepdims=True)
        y = o * lax.rsqrt(ms + EPS) * nw
        o_ref[b, sl, :] = (y * g_ref[b, sl, :].astype(F32)).astype(BF16)

    @pl.when(ti == pl.num_programs(2) - 1)
    def _():
        sout_ref[...] = s_scr[...]


def _rec_blocks(t_len, batch):
    tb = min(t_len, CHUNKS_PER_STEP * CHUNK)
    bb = 1 if t_len > CHUNK else min(batch, CHUNKS_PER_STEP)
    return bb, tb


def _recurrence(p, a, s0, norm_w, mall, lev, *, name, q_col, k_col, v_col, g_col):
    b, t, _ = p.shape
    _, heads, dk, dv = s0.shape
    bb, tb = _rec_blocks(t, b)
    blk = lambda width, col: pl.BlockSpec((bb, tb, width), lambda bi, h, ti: (bi, ti, col // width + h))
    const = lambda shape: pl.BlockSpec(shape, lambda bi, h, ti: (0,) * len(shape))
    st = pl.BlockSpec((bb, None, dk, dv), lambda bi, h, ti: (bi, h, 0, 0))
    return pl.pallas_call(
        functools.partial(_rec_kernel, bb=bb, tb=tb),
        grid=(b // bb, heads, t // tb),
        in_specs=[blk(dk, q_col), blk(dk, k_col), blk(dv, v_col), blk(dv, g_col),
                  blk(dk, 0), blk(dk, heads * dk), st, const((1, dv)), const(mall.shape), const(lev.shape)],
        out_specs=[blk(dv, 0), st],
        out_shape=[jax.ShapeDtypeStruct((b, t, heads * dv), BF16), jax.ShapeDtypeStruct(s0.shape, F32)],
        scratch_shapes=[pltpu.VMEM((bb, dk, dv), F32)],
        compiler_params=pltpu.CompilerParams(
            dimension_semantics=("parallel", "parallel", "arbitrary"), vmem_limit_bytes=VMEM_LIMIT),
        name=name,
    )(p, p, p, p, a, a, s0, norm_w, mall, lev)


def _outproj_kernel(ohg_ref, ogla_ref, wt_ref, wb_ref, x_ref, nw_ref, x1_ref, h2_ref):
    m = (jnp.dot(ohg_ref[...], wt_ref[...], preferred_element_type=F32)
         + jnp.dot(ogla_ref[...], wb_ref[...], preferred_element_type=F32))
    x1 = x_ref[...] + m
    x1_ref[...] = x1
    h2_ref[...] = _rms(x1, nw_ref[...]).astype(BF16)


def _outproj(o_hg, o_gla, w_top, w_bot, x, norm_w, tm):
    t, d = x.shape
    half = o_hg.shape[1]
    return pl.pallas_call(
        _outproj_kernel,
        grid=(t // tm,),
        in_specs=[
            pl.BlockSpec((tm, half), lambda i: (i, 0)),
            pl.BlockSpec((tm, half), lambda i: (i, 0)),
            pl.BlockSpec((half, d), lambda i: (0, 0)),
            pl.BlockSpec((half, d), lambda i: (0, 0)),
            pl.BlockSpec((tm, d), lambda i: (i, 0)),
            pl.BlockSpec((1, d), lambda i: (0, 0)),
        ],
        out_specs=[pl.BlockSpec((tm, d), lambda i: (i, 0)), pl.BlockSpec((tm, d), lambda i: (i, 0))],
        out_shape=[jax.ShapeDtypeStruct((t, d), F32), jax.ShapeDtypeStruct((t, d), BF16)],
        compiler_params=pltpu.CompilerParams(
            dimension_semantics=("parallel",), vmem_limit_bytes=VMEM_LIMIT),
        name="outproj",
    )(o_hg, o_gla, w_top, w_bot, x, norm_w)


def _ffn_kernel(h_ref, wg_ref, wu_ref, wd_ref, x1_ref, nw_ref, o_ref):
    j = pl.program_id(1)

    @pl.when(j == 0)
    def _():
        o_ref[...] = x1_ref[...]

    h = h_ref[...]
    g = jnp.dot(h, wg_ref[...], preferred_element_type=F32)
    u = jnp.dot(h, wu_ref[...], preferred_element_type=F32)
    mid = ((g * _sigmoid(g)) * u).astype(BF16)
    o_ref[...] += jnp.dot(mid, wd_ref[...], preferred_element_type=F32)

    @pl.when(j == pl.num_programs(1) - 1)
    def _():
        o_ref[...] = _rms(o_ref[...], nw_ref[...])


def _ffn(h2, w_gate_up, w_down, x1, norm_w, tm, tf):
    t, d = x1.shape
    nf = D_FF // tf
    return pl.pallas_call(
        _ffn_kernel,
        grid=(t // tm, nf),
        in_specs=[
            pl.BlockSpec((tm, d), lambda i, j: (i, 0)),
            pl.BlockSpec((d, tf), lambda i, j: (0, j)),
            pl.BlockSpec((d, tf), lambda i, j: (0, nf + j)),
            pl.BlockSpec((tf, d), lambda i, j: (j, 0)),
            pl.BlockSpec((tm, d), lambda i, j: (i, 0)),
            pl.BlockSpec((1, d), lambda i, j: (0, 0)),
        ],
        out_specs=pl.BlockSpec((tm, d), lambda i, j: (i, 0)),
        out_shape=jax.ShapeDtypeStruct((t, d), F32),
        compiler_params=pltpu.CompilerParams(
            dimension_semantics=("parallel", "arbitrary"), vmem_limit_bytes=VMEM_LIMIT),
        name="ffn",
    )(h2, w_gate_up, w_gate_up, w_down, x1, norm_w)


def _trunk(x, s_hg, s_gla, wts, consts):
    b, t, d = x.shape
    xf = x.reshape(b * t, d)
    mall, lev = consts
    p, a_hg, a_gla = _inproj(xf, wts["norm_mix"], wts["w_in"], wts["w_gr"], wts["lb_logits"], wts["w_gk2"],
                             wts["b_gk"], TM_INPROJ)
    p = p.reshape(b, t, -1)
    o_hg, s_hg_new = _recurrence(p, a_hg.reshape(b, t, -1), s_hg, wts["hg_norm"], mall, lev, name="hgrn",
                                 q_col=0, k_col=IN_TILE, v_col=2 * IN_TILE, g_col=3 * IN_TILE)
    gla_q = 4 * IN_TILE
    o_gla, s_gla_new = _recurrence(p, a_gla.reshape(b, t, -1), s_gla, wts["gla_norm"], mall, lev, name="gla",
                                   q_col=gla_q, k_col=gla_q + GLA_HEADS * GLA_DK, v_col=5 * IN_TILE,
                                   g_col=6 * IN_TILE)
    x1, h2 = _outproj(o_hg.reshape(b * t, -1), o_gla.reshape(b * t, -1), wts["w_out_top"], wts["w_out_bot"],
                      xf, wts["norm_ffn"], TM_OUTPROJ)
    y = _ffn(h2, wts["w_gate_up"], wts["w_down"], x1, wts["norm_final"], TM_FFN, TF_FFN)
    return y.reshape(b, t, d), s_hg_new[None], s_gla_new[None]


def kernel(x_prompt, x_sample, state_hgrn, state_gla, lb_logits, norm_mix, w_in, w_gk2, b_gk, hg_norm, gla_norm,
           w_out, norm_ffn, w_gate_up, w_down, norm_final):
    w = w_in[0].astype(BF16)
    hw = HG_HEADS * HG_DK
    w_gr = jnp.pad(w[:, P_COLS:P_COLS + GLA_RANK], ((0, 0), (0, LANES - GLA_RANK)))
    w2 = jnp.pad(w_gk2[0], ((0, LANES - GLA_RANK), (0, 0))).astype(BF16)
    wo = w_out[0].astype(BF16)
    wts = {
        "norm_mix": norm_mix[0][None, :], "w_in": w, "w_gr": w_gr,
        "lb_logits": lb_logits, "hg_norm": hg_norm[0][None, :],
        "w_gk2": w2, "b_gk": b_gk[0][None, :], "gla_norm": gla_norm[0][None, :],
        "w_out_top": wo[:hw], "w_out_bot": wo[hw:], "norm_ffn": norm_ffn[0][None, :],
        "w_gate_up": w_gate_up[0].astype(BF16), "w_down": w_down[0].astype(BF16),
        "norm_final": norm_final[None, :],
    }
    mall_np, lev_np = _chunk_constants()
    consts = (jnp.asarray(mall_np, BF16), jnp.asarray(lev_np))

    bp = x_prompt.shape[0]
    zero_hg = jnp.zeros((bp, HG_HEADS, HG_DK, HG_DV), F32)
    zero_gla = jnp.zeros((bp, GLA_HEADS, GLA_DK, GLA_DV), F32)
    y_p, hg_p, gla_p = _trunk(x_prompt, zero_hg, zero_gla, wts, consts)
    y_s, hg_s, gla_s = _trunk(x_sample, state_hgrn[0], state_gla[0], wts, consts)
    return (y_p, y_s, hg_p, gla_p, hg_s, gla_s)
```

```python
import functools

import numpy as np
import jax
import jax.numpy as jnp
from jax import lax
from jax.experimental import pallas as pl
from jax.experimental.pallas import tpu as pltpu

F32 = jnp.float32
BF16 = jnp.bfloat16

D_MODEL = 2048
HG_HEADS = 8
HG_DK = 128
HG_DV = 128
GLA_HEADS = 4
GLA_DK = 128
GLA_DV = 256
GLA_RANK = 16
GLA_GATE_NORMALIZER = 16.0
D_FF = 5632
EPS = 1e-6
LOG2E = 1.4426950408889634
LOG2_DECAY_FLOOR = -1e30

LANES = 128
CHUNK = 128
LEVELS = 7
MXU_LEVELS = 3
CHUNKS_PER_STEP = 32
VMEM_LIMIT = 60 * 1024 * 1024
TM_INPROJ = 1024
TM_OUTPROJ = 512
TM_FFN = 1024
TF_FFN = 512

IN_TILE = 1024
IN_TILES = 7
IN_SUB = 256
P_COLS = IN_TILE * IN_TILES


def _rms(x, w):
    ms = jnp.mean(x * x, axis=-1, keepdims=True)
    return (x * lax.rsqrt(ms + EPS)) * w


def _sigmoid(x):
    return 1.0 / (1.0 + jnp.exp(-x))


def _silu(x):
    return x * _sigmoid(x)


def _inproj_kernel(x_ref, nw_ref, w_ref, wr_ref, lbl_ref, w2_ref, b2_ref, p_ref, ahg_ref, agla_ref, h_scr):
    j = pl.program_id(1)
    hg = HG_HEADS * HG_DK
    gl = GLA_HEADS * GLA_DK

    @pl.when(j == 0)
    def _():
        h_scr[...] = _rms(x_ref[...], nw_ref[...]).astype(BF16)

    def forget_gate(z, cols):
        l0 = lbl_ref[0:1, cols]
        l1 = lbl_ref[1:2, cols]
        mx = jnp.maximum(l0, l1)
        e0 = jnp.exp(l0 - mx)
        e1 = jnp.exp(l1 - mx)
        lb = e0 / (e0 + e1)
        ez = jnp.exp(-jnp.abs(z))
        big = 1.0 / (1.0 + ez)
        small = ez * big
        pos = z >= 0.0
        a = jnp.maximum(jnp.log(lb + (1.0 - lb) * jnp.where(pos, big, small)) * LOG2E, LOG2_DECAY_FLOOR)
        hi = a.astype(BF16)
        ahg_ref[:, cols] = hi
        ahg_ref[:, hg + cols.start:hg + cols.stop] = (a - hi.astype(F32)).astype(BF16)
        return (1.0 - lb) * jnp.where(pos, small, big)

    activations = (
        lambda p, cols: _silu(p) * (HG_DK ** -0.5),
        forget_gate,
        lambda p, cols: p,
        lambda p, cols: _silu(p),
        lambda p, cols: p * ((GLA_DK ** -0.5) if cols.stop <= gl else 1.0),
        lambda p, cols: p,
        lambda p, cols: _silu(p),
    )

    def low_rank_gate():
        gr = jnp.dot(h_scr[...], wr_ref[...], preferred_element_type=F32).astype(BF16)
        x2 = (jnp.dot(gr, w2_ref[...], preferred_element_type=F32) + b2_ref[...]) * LOG2E
        a = (jnp.minimum(x2, 0.0) - jnp.log(1.0 + jnp.exp2(-jnp.abs(x2))) * LOG2E) * (1.0 / GLA_GATE_NORMALIZER)
        hi = a.astype(BF16)
        agla_ref[:, :gl] = hi
        agla_ref[:, gl:] = (a - hi.astype(F32)).astype(BF16)

    for tile in range(IN_TILES):
        @pl.when(j == tile)
        def _(tile=tile):
            for c0 in range(0, IN_TILE, IN_SUB):
                cols = slice(c0, c0 + IN_SUB)
                p = jnp.dot(h_scr[...], w_ref[:, cols], preferred_element_type=F32)
                p_ref[:, cols] = activations[tile](p, cols).astype(BF16)
            if tile == IN_TILES - 1:
                low_rank_gate()


def _inproj(x, norm_w, w, w_gr, lb_logits, w2, b2, tm):
    t, d = x.shape
    hg = HG_HEADS * HG_DK
    gl = GLA_HEADS * GLA_DK
    const = lambda shape: pl.BlockSpec(shape, lambda i, j: (0,) * len(shape))
    return pl.pallas_call(
        _inproj_kernel,
        grid=(t // tm, IN_TILES),
        in_specs=[
            pl.BlockSpec((tm, d), lambda i, j: (i, 0)),
            const((1, d)),
            pl.BlockSpec((d, IN_TILE), lambda i, j: (0, j)),
            const((d, LANES)), const((2, hg)), const((LANES, gl)), const((1, gl)),
        ],
        out_specs=[pl.BlockSpec((tm, IN_TILE), lambda i, j: (i, j)),
                   pl.BlockSpec((tm, 2 * hg), lambda i, j: (i, 0)),
                   pl.BlockSpec((tm, 2 * gl), lambda i, j: (i, 0))],
        out_shape=[jax.ShapeDtypeStruct((t, P_COLS), BF16),
                   jax.ShapeDtypeStruct((t, 2 * hg), BF16),
                   jax.ShapeDtypeStruct((t, 2 * gl), BF16)],
        scratch_shapes=[pltpu.VMEM((tm, d), BF16)],
        compiler_params=pltpu.CompilerParams(
            dimension_semantics=("parallel", "arbitrary"), vmem_limit_bytes=VMEM_LIMIT),
        name="inproj",
    )(x, norm_w, w, w_gr, lb_logits, w2, b2)


def _chunk_constants():
    c = CHUNK
    t = np.arange(c)[:, None]
    u = np.arange(c)[None, :]
    mats = []
    for l in range(MXU_LEVELS):
        h = 1 << l
        mid = (t // (2 * h)) * (2 * h) + h - 1
        upper = (t & h) != 0
        mats.append(np.where(upper, (u > mid) & (u <= t), (u > t) & (u <= mid)))
    mats.append(u <= t)
    mall = np.concatenate(mats, axis=0).astype(np.float32)
    mall = np.concatenate([mall, mall], axis=1)
    x = t ^ u
    lev = np.where(t > u, np.floor(np.log2(np.maximum(x, 1))).astype(np.int32), -1)
    lev = np.where(t == u, LEVELS, lev).astype(np.int32)
    return mall, lev


def _aligned_operands(q, k, cum, l):
    h = 1 << l
    lows, ups = [], []
    for base in range(0, CHUNK, 2 * h):
        mid = cum[base + h - 1:base + h, :]
        lows.append(k[base:base + h] * jnp.exp2(mid - cum[base:base + h]))
        ups.append(q[base + h:base + 2 * h] * jnp.exp2(cum[base + h:base + 2 * h] - mid))
    x_up = jnp.concatenate(ups, axis=0).astype(BF16)
    x = jnp.concatenate([z for pair in zip(lows, ups) for z in pair], axis=0)
    return x_up, x.T.astype(BF16)


def _aligned_update(scores, sc, member, l):
    h = 1 << l
    parts = []
    for i, base in enumerate(range(0, CHUNK, 2 * h)):
        rows = slice(base + h, base + 2 * h)
        parts.append(scores[base:base + h])
        parts.append(jnp.where(member[rows], sc[i * h:(i + 1) * h], scores[rows]))
    return jnp.concatenate(parts, axis=0)


def _chunks_parallel(qs, ks, as_, vs, mall, lev):
    c = CHUNK
    n = len(qs)
    dv = vs[0].shape[1]
    e_all = jnp.dot(mall, jnp.concatenate(as_, axis=1), preferred_element_type=F32)
    es = [e_all[:, i * LANES:(i + 1) * LANES] for i in range(n)]
    cums = [e[MXU_LEVELS * c:] for e in es]

    row = lax.broadcasted_iota(jnp.int32, (c, LANES), 0)
    uppers = [(row & (1 << l)) != 0 for l in range(MXU_LEVELS)]
    tots = [cum[c - 1:c, :] for cum in cums]
    operands, qcs, krs = [], [], []
    for q, k, e, cum, tot in zip(qs, ks, es, cums, tots):
        ops = []
        for l in range(MXU_LEVELS):
            x = jnp.where(uppers[l], q, k) * jnp.exp2(e[l * c:(l + 1) * c])
            ops.append((x.astype(BF16), x.T.astype(BF16)))
        for l in range(MXU_LEVELS, LEVELS):
            ops.append(_aligned_operands(q, k, cum, l))
        operands.append(ops)
        qcs.append((q * jnp.exp2(cum)).astype(BF16))
        krs.append((k * jnp.exp2(tot - cum)).T.astype(BF16))
    products = [[jnp.dot(x_up, x_t, preferred_element_type=F32) for x_up, x_t in ops]
                for ops in operands]
    members = [lev == l for l in range(LEVELS)]
    scores = []
    for q, k, scs in zip(qs, ks, products):
        s = jnp.where(lev == LEVELS, jnp.sum(q * k, axis=1, keepdims=True), 0.0)
        for l in range(MXU_LEVELS):
            s = jnp.where(members[l], scs[l], s)
        for l in range(MXU_LEVELS, LEVELS):
            s = _aligned_update(s, scs[l], members[l], l)
        scores.append(s.astype(BF16))

    both = [jnp.dot(jnp.concatenate([s, kr_t], axis=0), v, preferred_element_type=F32)
            for s, kr_t, v in zip(scores, krs, vs)]
    ovs = [z[:c] for z in both]
    us = [z[c:] for z in both]
    dcols = [jnp.broadcast_to(jnp.exp2(tot), (LANES, LANES)).T for tot in tots]
    if dv > LANES:
        dcols = [jnp.concatenate([d] * (dv // LANES), axis=1) for d in dcols]
    return qcs, ovs, us, dcols


def _pad_rows(x, rows):
    if x.shape[0] == rows:
        return x
    return jnp.concatenate([x, jnp.zeros((rows - x.shape[0], x.shape[1]), x.dtype)], axis=0)


def _rec_kernel(q_ref, k_ref, v_ref, g_ref, ahi_ref, alo_ref, s0_ref, nw_ref, mall_ref, lev_ref,
                o_ref, sout_ref, s_scr, *, bb, tb):
    ti = pl.program_id(2)

    @pl.when(ti == 0)
    def _():
        s_scr[...] = s0_ref[...]

    rows = min(tb, CHUNK)
    items = [(b, slice(r0, r0 + rows)) for b in range(bb) for r0 in range(0, tb, rows)]
    qs = [_pad_rows(q_ref[b, sl, :].astype(F32), CHUNK) for b, sl in items]
    ks = [_pad_rows(k_ref[b, sl, :].astype(F32), CHUNK) for b, sl in items]
    vs = [_pad_rows(v_ref[b, sl, :], CHUNK) for b, sl in items]
    as_ = [jnp.concatenate([_pad_rows(ahi_ref[b, sl, :], CHUNK), _pad_rows(alo_ref[b, sl, :], CHUNK)], axis=0)
           for b, sl in items]
    qcs, ovs, us, dcols = _chunks_parallel(qs, ks, as_, vs, mall_ref[...], lev_ref[...])

    states = []
    for i, (b, sl) in enumerate(items):
        s = s_scr[b] if sl.start == 0 else s
        states.append(s)
        s = dcols[i] * s + us[i]
        if sl.stop == tb:
            s_scr[b] = s
    outs = [jnp.dot(qc, s.astype(BF16), preferred_element_type=F32) + ov for qc, s, ov in zip(qcs, states, ovs)]

    nw = nw_ref[...]
    for (b, sl), o in zip(items, outs):
        o = o[:rows]
        ms = jnp.mean(o * o, axis=-1, keepdims=True)
        y = o * lax.rsqrt(ms + EPS) * nw
        o_ref[b, sl, :] = (y * g_ref[b, sl, :].astype(F32)).astype(BF16)

    @pl.when(ti == pl.num_programs(2) - 1)
    def _():
        sout_ref[...] = s_scr[...]


def _rec_blocks(t_len, batch):
    tb = min(t_len, CHUNKS_PER_STEP * CHUNK)
    bb = 1 if t_len > CHUNK else min(batch, CHUNKS_PER_STEP)
    return bb, tb


def _recurrence(p, a, s0, norm_w, mall, lev, *, name, q_col, k_col, v_col, g_col):
    b, t, _ = p.shape
    _, heads, dk, dv = s0.shape
    bb, tb = _rec_blocks(t, b)
    blk = lambda width, col: pl.BlockSpec((bb, tb, width), lambda bi, h, ti: (bi, ti, col // width + h))
    const = lambda shape: pl.BlockSpec(shape, lambda bi, h, ti: (0,) * len(shape))
    st = pl.BlockSpec((bb, None, dk, dv), lambda bi, h, ti: (bi, h, 0, 0))
    return pl.pallas_call(
        functools.partial(_rec_kernel, bb=bb, tb=tb),
        grid=(b // bb, heads, t // tb),
        in_specs=[blk(dk, q_col), blk(dk, k_col), blk(dv, v_col), blk(dv, g_col),
                  blk(dk, 0), blk(dk, heads * dk), st, const((1, dv)), const(mall.shape), const(lev.shape)],
        out_specs=[blk(dv, 0), st],
        out_shape=[jax.ShapeDtypeStruct((b, t, heads * dv), BF16), jax.ShapeDtypeStruct(s0.shape, F32)],
        scratch_shapes=[pltpu.VMEM((bb, dk, dv), F32)],
        compiler_params=pltpu.CompilerParams(
            dimension_semantics=("parallel", "parallel", "arbitrary"), vmem_limit_bytes=VMEM_LIMIT),
        name=name,
    )(p, p, p, p, a, a, s0, norm_w, mall, lev)


def _outproj_kernel(ohg_ref, ogla_ref, wt_ref, wb_ref, x_ref, nw_ref, x1_ref, h2_ref):
    m = (jnp.dot(ohg_ref[...], wt_ref[...], preferred_element_type=F32)
         + jnp.dot(ogla_ref[...], wb_ref[...], preferred_element_type=F32))
    x1 = x_ref[...] + m
    x1_ref[...] = x1
    h2_ref[...] = _rms(x1, nw_ref[...]).astype(BF16)


def _outproj(o_hg, o_gla, w_top, w_bot, x, norm_w, tm):
    t, d = x.shape
    half = o_hg.shape[1]
    return pl.pallas_call(
        _outproj_kernel,
        grid=(t // tm,),
        in_specs=[
            pl.BlockSpec((tm, half), lambda i: (i, 0)),
            pl.BlockSpec((tm, half), lambda i: (i, 0)),
            pl.BlockSpec((half, d), lambda i: (0, 0)),
            pl.BlockSpec((half, d), lambda i: (0, 0)),
            pl.BlockSpec((tm, d), lambda i: (i, 0)),
            pl.BlockSpec((1, d), lambda i: (0, 0)),
        ],
        out_specs=[pl.BlockSpec((tm, d), lambda i: (i, 0)), pl.BlockSpec((tm, d), lambda i: (i, 0))],
        out_shape=[jax.ShapeDtypeStruct((t, d), F32), jax.ShapeDtypeStruct((t, d), BF16)],
        compiler_params=pltpu.CompilerParams(
            dimension_semantics=("parallel",), vmem_limit_bytes=VMEM_LIMIT),
        name="outproj",
    )(o_hg, o_gla, w_top, w_bot, x, norm_w)


def _ffn_kernel(h_ref, wg_ref, wu_ref, wd_ref, x1_ref, nw_ref, o_ref):
    j = pl.program_id(1)

    @pl.when(j == 0)
    def _():
        o_ref[...] = x1_ref[...]

    h = h_ref[...]
    g = jnp.dot(h, wg_ref[...], preferred_element_type=F32)
    u = jnp.dot(h, wu_ref[...], preferred_element_type=F32)
    mid = ((g * _sigmoid(g)) * u).astype(BF16)
    o_ref[...] += jnp.dot(mid, wd_ref[...], preferred_element_type=F32)

    @pl.when(j == pl.num_programs(1) - 1)
    def _():
        o_ref[...] = _rms(o_ref[...], nw_ref[...])


def _ffn(h2, w_gate_up, w_down, x1, norm_w, tm, tf):
    t, d = x1.shape
    nf = D_FF // tf
    return pl.pallas_call(
        _ffn_kernel,
        grid=(t // tm, nf),
        in_specs=[
            pl.BlockSpec((tm, d), lambda i, j: (i, 0)),
            pl.BlockSpec((d, tf), lambda i, j: (0, j)),
            pl.BlockSpec((d, tf), lambda i, j: (0, nf + j)),
            pl.BlockSpec((tf, d), lambda i, j: (j, 0)),
            pl.BlockSpec((tm, d), lambda i, j: (i, 0)),
            pl.BlockSpec((1, d), lambda i, j: (0, 0)),
        ],
        out_specs=pl.BlockSpec((tm, d), lambda i, j: (i, 0)),
        out_shape=jax.ShapeDtypeStruct((t, d), F32),
        compiler_params=pltpu.CompilerParams(
            dimension_semantics=("parallel", "arbitrary"), vmem_limit_bytes=VMEM_LIMIT),
        name="ffn",
    )(h2, w_gate_up, w_gate_up, w_down, x1, norm_w)


def _trunk(x, s_hg, s_gla, wts, consts):
    b, t, d = x.shape
    xf = x.reshape(b * t, d)
    mall, lev = consts
    p, a_hg, a_gla = _inproj(xf, wts["norm_mix"], wts["w_in"], wts["w_gr"], wts["lb_logits"], wts["w_gk2"],
                             wts["b_gk"], TM_INPROJ)
    p = p.reshape(b, t, -1)
    o_hg, s_hg_new = _recurrence(p, a_hg.reshape(b, t, -1), s_hg, wts["hg_norm"], mall, lev, name="hgrn",
                                 q_col=0, k_col=IN_TILE, v_col=2 * IN_TILE, g_col=3 * IN_TILE)
    gla_q = 4 * IN_TILE
    o_gla, s_gla_new = _recurrence(p, a_gla.reshape(b, t, -1), s_gla, wts["gla_norm"], mall, lev, name="gla",
                                   q_col=gla_q, k_col=gla_q + GLA_HEADS * GLA_DK, v_col=5 * IN_TILE,
                                   g_col=6 * IN_TILE)
    x1, h2 = _outproj(o_hg.reshape(b * t, -1), o_gla.reshape(b * t, -1), wts["w_out_top"], wts["w_out_bot"],
                      xf, wts["norm_ffn"], TM_OUTPROJ)
    y = _ffn(h2, wts["w_gate_up"], wts["w_down"], x1, wts["norm_final"], TM_FFN, TF_FFN)
    return y.reshape(b, t, d), s_hg_new[None], s_gla_new[None]


def kernel(x_prompt, x_sample, state_hgrn, state_gla, lb_logits, norm_mix, w_in, w_gk2, b_gk, hg_norm, gla_norm,
           w_out, norm_ffn, w_gate_up, w_down, norm_final):
    w = w_in[0].astype(BF16)
    hw = HG_HEADS * HG_DK
    w_gr = jnp.pad(w[:, P_COLS:P_COLS + GLA_RANK], ((0, 0), (0, LANES - GLA_RANK)))
    w2 = jnp.pad(w_gk2[0], ((0, LANES - GLA_RANK), (0, 0))).astype(BF16)
    wo = w_out[0].astype(BF16)
    wts = {
        "norm_mix": norm_mix[0][None, :], "w_in": w, "w_gr": w_gr,
        "lb_logits": lb_logits, "hg_norm": hg_norm[0][None, :],
        "w_gk2": w2, "b_gk": b_gk[0][None, :], "gla_norm": gla_norm[0][None, :],
        "w_out_top": wo[:hw], "w_out_bot": wo[hw:], "norm_ffn": norm_ffn[0][None, :],
        "w_gate_up": w_gate_up[0].astype(BF16), "w_down": w_down[0].astype(BF16),
        "norm_final": norm_final[None, :],
    }
    mall_np, lev_np = _chunk_constants()
    consts = (jnp.asarray(mall_np, BF16), jnp.asarray(lev_np))

    bp = x_prompt.shape[0]
    zero_hg = jnp.zeros((bp, HG_HEADS, HG_DK, HG_DV), F32)
    zero_gla = jnp.zeros((bp, GLA_HEADS, GLA_DK, GLA_DV), F32)
    y_p, hg_p, gla_p = _trunk(x_prompt, zero_hg, zero_gla, wts, consts)
    y_s, hg_s, gla_s = _trunk(x_sample, state_hgrn[0], state_gla[0], wts, consts)
    return (y_p, y_s, hg_p, gla_p, hg_s, gla_s)
```

```python
import functools

import numpy as np
import jax
import jax.numpy as jnp
from jax import lax
from jax.experimental import pallas as pl
from jax.experimental.pallas import tpu as pltpu

F32 = jnp.float32
BF16 = jnp.bfloat16

HG_HEADS = 8
HG_DK = 128
HG_DV = 128
GLA_HEADS = 4
GLA_DK = 128
GLA_DV = 256
GLA_RANK = 16
GLA_GATE_NORMALIZER = 16.0
D_FF = 5632
EPS = 1e-6
LOG2E = 1.4426950408889634
LOG2_DECAY_FLOOR = -1e30

LANES = 128
CHUNK = 128
LEVELS = 7
MXU_LEVELS = 3
CHUNKS_PER_STEP = 32
VMEM_LIMIT = 60 * 1024 * 1024
TM_INPROJ = 1024
TM_OUTPROJ = 512
TM_FFN = 1024
TF_FFN = 512

IN_TILE = 1024
IN_TILES = 7
P_COLS = IN_TILE * IN_TILES


def _rms(x, w):
    ms = jnp.mean(x * x, axis=-1, keepdims=True)
    return (x * lax.rsqrt(ms + EPS)) * w


def _sigmoid(x):
    return 1.0 / (1.0 + jnp.exp(-x))


def _silu(x):
    return x * _sigmoid(x)


def _inproj_kernel(x_ref, nw_ref, w_ref, wr_ref, lbl_ref, w2_ref, b2_ref, p_ref, ahg_ref, agla_ref, h_scr):
    j = pl.program_id(1)
    hg = HG_HEADS * HG_DK
    gl = GLA_HEADS * GLA_DK

    @pl.when(j == 0)
    def _():
        h_scr[...] = _rms(x_ref[...], nw_ref[...]).astype(BF16)

    def forget_gate(z, cols):
        l0 = lbl_ref[0:1, cols]
        l1 = lbl_ref[1:2, cols]
        mx = jnp.maximum(l0, l1)
        e0 = jnp.exp(l0 - mx)
        e1 = jnp.exp(l1 - mx)
        lb = e0 / (e0 + e1)
        ez = jnp.exp(-jnp.abs(z))
        big = 1.0 / (1.0 + ez)
        small = ez * big
        pos = z >= 0.0
        a = jnp.maximum(jnp.log(lb + (1.0 - lb) * jnp.where(pos, big, small)) * LOG2E, LOG2_DECAY_FLOOR)
        hi = a.astype(BF16)
        ahg_ref[:, cols] = hi
        ahg_ref[:, hg + cols.start:hg + cols.stop] = (a - hi.astype(F32)).astype(BF16)
        return (1.0 - lb) * jnp.where(pos, small, big)

    whole = slice(0, IN_TILE)
    identity = lambda p, cols: p
    silu = lambda p, cols: _silu(p)
    pieces = (
        ((whole, lambda p, cols: _silu(p) * (HG_DK ** -0.5)),),
        ((whole, forget_gate),),
        ((whole, identity),),
        ((whole, silu),),
        ((slice(0, gl), lambda p, cols: p * (GLA_DK ** -0.5)), (slice(gl, IN_TILE), identity)),
        ((whole, identity),),
        ((whole, silu),),
    )

    def low_rank_gate():
        gr = jnp.dot(h_scr[...], wr_ref[...], preferred_element_type=F32).astype(BF16)
        x2 = (jnp.dot(gr, w2_ref[...], preferred_element_type=F32) + b2_ref[...]) * LOG2E
        a = (jnp.minimum(x2, 0.0) - jnp.log(1.0 + jnp.exp2(-jnp.abs(x2))) * LOG2E) * (1.0 / GLA_GATE_NORMALIZER)
        hi = a.astype(BF16)
        agla_ref[:, :gl] = hi
        agla_ref[:, gl:] = (a - hi.astype(F32)).astype(BF16)

    for tile in range(IN_TILES):
        @pl.when(j == tile)
        def _(tile=tile):
            for cols, activation in pieces[tile]:
                p = jnp.dot(h_scr[...], w_ref[:, cols], preferred_element_type=F32)
                p_ref[:, cols] = activation(p, cols).astype(BF16)
            if tile == IN_TILES - 1:
                low_rank_gate()


def _inproj(x, norm_w, w, w_gr, lb_logits, w2, b2, tm):
    t, d = x.shape
    hg = HG_HEADS * HG_DK
    gl = GLA_HEADS * GLA_DK
    const = lambda shape: pl.BlockSpec(shape, lambda i, j: (0,) * len(shape))
    return pl.pallas_call(
        _inproj_kernel,
        grid=(t // tm, IN_TILES),
        in_specs=[
            pl.BlockSpec((tm, d), lambda i, j: (i, 0)),
            const((1, d)),
            pl.BlockSpec((d, IN_TILE), lambda i, j: (0, j)),
            const((d, LANES)), const((2, hg)), const((LANES, gl)), const((1, gl)),
        ],
        out_specs=[pl.BlockSpec((tm, IN_TILE), lambda i, j: (i, j)),
                   pl.BlockSpec((tm, 2 * hg), lambda i, j: (i, 0)),
                   pl.BlockSpec((tm, 2 * gl), lambda i, j: (i, 0))],
        out_shape=[jax.ShapeDtypeStruct((t, P_COLS), BF16),
                   jax.ShapeDtypeStruct((t, 2 * hg), BF16),
                   jax.ShapeDtypeStruct((t, 2 * gl), BF16)],
        scratch_shapes=[pltpu.VMEM((tm, d), BF16)],
        compiler_params=pltpu.CompilerParams(
            dimension_semantics=("parallel", "arbitrary"), vmem_limit_bytes=VMEM_LIMIT),
        name="inproj",
    )(x, norm_w, w, w_gr, lb_logits, w2, b2)


def _chunk_constants():
    c = CHUNK
    t = np.arange(c)[:, None]
    u = np.arange(c)[None, :]
    mats = []
    for l in range(MXU_LEVELS):
        h = 1 << l
        mid = (t // (2 * h)) * (2 * h) + h - 1
        upper = (t & h) != 0
        mats.append(np.where(upper, (u > mid) & (u <= t), (u > t) & (u <= mid)))
    mats.append(u <= t)
    mall = np.concatenate(mats, axis=0).astype(np.float32)
    mall = np.concatenate([mall, mall], axis=1)
    x = t ^ u
    lev = np.where(t > u, np.floor(np.log2(np.maximum(x, 1))).astype(np.int32), -1)
    lev = np.where(t == u, LEVELS, lev).astype(np.int32)
    return mall, lev


def _aligned_operands(q, k, cum, l):
    h = 1 << l
    lows, ups = [], []
    for base in range(0, CHUNK, 2 * h):
        mid = cum[base + h - 1:base + h, :]
        lows.append(k[base:base + h] * jnp.exp2(mid - cum[base:base + h]))
        ups.append(q[base + h:base + 2 * h] * jnp.exp2(cum[base + h:base + 2 * h] - mid))
    x_up = jnp.concatenate(ups, axis=0).astype(BF16)
    x = jnp.concatenate([z for pair in zip(lows, ups) for z in pair], axis=0)
    return x_up, x.T.astype(BF16)


def _aligned_update(scores, sc, member, l):
    h = 1 << l
    parts = []
    for i, base in enumerate(range(0, CHUNK, 2 * h)):
        rows = slice(base + h, base + 2 * h)
        parts.append(scores[base:base + h])
        parts.append(jnp.where(member[rows], sc[i * h:(i + 1) * h], scores[rows]))
    return jnp.concatenate(parts, axis=0)


def _chunks_parallel(qs, ks, as_, vs, mall, lev):
    c = CHUNK
    n = len(qs)
    dv = vs[0].shape[1]
    e_all = jnp.dot(mall, jnp.concatenate(as_, axis=1), preferred_element_type=F32)
    es = [e_all[:, i * LANES:(i + 1) * LANES] for i in range(n)]
    cums = [e[MXU_LEVELS * c:] for e in es]

    row = lax.broadcasted_iota(jnp.int32, (c, LANES), 0)
    uppers = [(row & (1 << l)) != 0 for l in range(MXU_LEVELS)]
    tots = [cum[c - 1:c, :] for cum in cums]
    operands, qcs, krs = [], [], []
    for q, k, e, cum, tot in zip(qs, ks, es, cums, tots):
        ops = []
        for l in range(MXU_LEVELS):
            x = jnp.where(uppers[l], q, k) * jnp.exp2(e[l * c:(l + 1) * c])
            ops.append((x.astype(BF16), x.T.astype(BF16)))
        for l in range(MXU_LEVELS, LEVELS):
            ops.append(_aligned_operands(q, k, cum, l))
        operands.append(ops)
        qcs.append((q * jnp.exp2(cum)).astype(BF16))
        krs.append((k * jnp.exp2(tot - cum)).T.astype(BF16))
    products = [[jnp.dot(x_up, x_t, preferred_element_type=F32) for x_up, x_t in ops]
                for ops in operands]
    members = [lev == l for l in range(LEVELS)]
    scores = []
    for q, k, scs in zip(qs, ks, products):
        s = jnp.where(lev == LEVELS, jnp.sum(q * k, axis=1, keepdims=True), 0.0)
        for l in range(MXU_LEVELS):
            s = jnp.where(members[l], scs[l], s)
        for l in range(MXU_LEVELS, LEVELS):
            s = _aligned_update(s, scs[l], members[l], l)
        scores.append(s.astype(BF16))

    both = [jnp.dot(jnp.concatenate([s, kr_t], axis=0), v, preferred_element_type=F32)
            for s, kr_t, v in zip(scores, krs, vs)]
    ovs = [z[:c] for z in both]
    us = [z[c:] for z in both]
    dcols = [jnp.broadcast_to(jnp.exp2(tot), (LANES, LANES)).T for tot in tots]
    if dv > LANES:
        dcols = [jnp.concatenate([d] * (dv // LANES), axis=1) for d in dcols]
    return qcs, ovs, us, dcols


def _pad_rows(x, rows):
    if x.shape[0] == rows:
        return x
    return jnp.concatenate([x, jnp.zeros((rows - x.shape[0], x.shape[1]), x.dtype)], axis=0)


def _rec_kernel(q_ref, k_ref, v_ref, g_ref, ahi_ref, alo_ref, s0_ref, nw_ref, mall_ref, lev_ref,
                o_ref, sout_ref, s_scr, *, bb, tb):
    ti = pl.program_id(2)

    @pl.when(ti == 0)
    def _():
        s_scr[...] = s0_ref[...]

    rows = min(tb, CHUNK)
    items = [(b, slice(r0, r0 + rows)) for b in range(bb) for r0 in range(0, tb, rows)]
    qs = [_pad_rows(q_ref[b, sl, :].astype(F32), CHUNK) for b, sl in items]
    ks = [_pad_rows(k_ref[b, sl, :].astype(F32), CHUNK) for b, sl in items]
    vs = [_pad_rows(v_ref[b, sl, :], CHUNK) for b, sl in items]
    as_ = [jnp.concatenate([_pad_rows(ahi_ref[b, sl, :], CHUNK), _pad_rows(alo_ref[b, sl, :], CHUNK)], axis=0)
           for b, sl in items]
    qcs, ovs, us, dcols = _chunks_parallel(qs, ks, as_, vs, mall_ref[...], lev_ref[...])

    states = []
    for i, (b, sl) in enumerate(items):
        s = s_scr[b] if sl.start == 0 else s
        states.append(s)
        s = dcols[i] * s + us[i]
        if sl.stop == tb:
            s_scr[b] = s
    outs = [jnp.dot(qc, s.astype(BF16), preferred_element_type=F32) + ov for qc, s, ov in zip(qcs, states, ovs)]

    nw = nw_ref[...]
    for (b, sl), o in zip(items, outs):
        o = o[:rows]
        ms = jnp.mean(o * o, axis=-1, keepdims=True)
        y = o * lax.rsqrt(ms + EPS) * nw
        o_ref[b, sl, :] = (y * g_ref[b, sl, :].astype(F32)).astype(BF16)

    @pl.when(ti == pl.num_programs(2) - 1)
    def _():
        sout_ref[...] = s_scr[...]


def _rec_blocks(t_len, batch):
    tb = min(t_len, CHUNKS_PER_STEP * CHUNK)
    bb = 1 if t_len > CHUNK else min(batch, CHUNKS_PER_STEP)
    return bb, tb


def _recurrence(p, a, s0, norm_w, mall, lev, *, name, q_col, k_col, v_col, g_col):
    b, t, _ = p.shape
    _, heads, dk, dv = s0.shape
    bb, tb = _rec_blocks(t, b)
    blk = lambda width, col: pl.BlockSpec((bb, tb, width), lambda bi, h, ti: (bi, ti, col // width + h))
    const = lambda shape: pl.BlockSpec(shape, lambda bi, h, ti: (0,) * len(shape))
    st = pl.BlockSpec((bb, None, dk, dv), lambda bi, h, ti: (bi, h, 0, 0))
    return pl.pallas_call(
        functools.partial(_rec_kernel, bb=bb, tb=tb),
        grid=(b // bb, heads, t // tb),
        in_specs=[blk(dk, q_col), blk(dk, k_col), blk(dv, v_col), blk(dv, g_col),
                  blk(dk, 0), blk(dk, heads * dk), st, const((1, dv)), const(mall.shape), const(lev.shape)],
        out_specs=[blk(dv, 0), st],
        out_shape=[jax.ShapeDtypeStruct((b, t, heads * dv), BF16), jax.ShapeDtypeStruct(s0.shape, F32)],
        scratch_shapes=[pltpu.VMEM((bb, dk, dv), F32)],
        compiler_params=pltpu.CompilerParams(
            dimension_semantics=("parallel", "parallel", "arbitrary"), vmem_limit_bytes=VMEM_LIMIT),
        name=name,
    )(p, p, p, p, a, a, s0, norm_w, mall, lev)


def _outproj_kernel(ohg_ref, ogla_ref, wt_ref, wb_ref, x_ref, nw_ref, x1_ref, h2_ref):
    m = (jnp.dot(ohg_ref[...], wt_ref[...], preferred_element_type=F32)
         + jnp.dot(ogla_ref[...], wb_ref[...], preferred_element_type=F32))
    x1 = x_ref[...] + m
    x1_ref[...] = x1
    h2_ref[...] = _rms(x1, nw_ref[...]).astype(BF16)


def _outproj(o_hg, o_gla, w_top, w_bot, x, norm_w, tm):
    t, d = x.shape
    half = o_hg.shape[1]
    return pl.pallas_call(
        _outproj_kernel,
        grid=(t // tm,),
        in_specs=[
            pl.BlockSpec((tm, half), lambda i: (i, 0)),
            pl.BlockSpec((tm, half), lambda i: (i, 0)),
            pl.BlockSpec((half, d), lambda i: (0, 0)),
            pl.BlockSpec((half, d), lambda i: (0, 0)),
            pl.BlockSpec((tm, d), lambda i: (i, 0)),
            pl.BlockSpec((1, d), lambda i: (0, 0)),
        ],
        out_specs=[pl.BlockSpec((tm, d), lambda i: (i, 0)), pl.BlockSpec((tm, d), lambda i: (i, 0))],
        out_shape=[jax.ShapeDtypeStruct((t, d), F32), jax.ShapeDtypeStruct((t, d), BF16)],
        compiler_params=pltpu.CompilerParams(
            dimension_semantics=("parallel",), vmem_limit_bytes=VMEM_LIMIT),
        name="outproj",
    )(o_hg, o_gla, w_top, w_bot, x, norm_w)


def _ffn_kernel(h_ref, wg_ref, wu_ref, wd_ref, x1_ref, nw_ref, o_ref):
    j = pl.program_id(1)

    @pl.when(j == 0)
    def _():
        o_ref[...] = x1_ref[...]

    h = h_ref[...]
    g = jnp.dot(h, wg_ref[...], preferred_element_type=F32)
    u = jnp.dot(h, wu_ref[...], preferred_element_type=F32)
    mid = ((g * _sigmoid(g)) * u).astype(BF16)
    o_ref[...] += jnp.dot(mid, wd_ref[...], preferred_element_type=F32)

    @pl.when(j == pl.num_programs(1) - 1)
    def _():
        o_ref[...] = _rms(o_ref[...], nw_ref[...])


def _ffn(h2, w_gate_up, w_down, x1, norm_w, tm, tf):
    t, d = x1.shape
    nf = D_FF // tf
    return pl.pallas_call(
        _ffn_kernel,
        grid=(t // tm, nf),
        in_specs=[
            pl.BlockSpec((tm, d), lambda i, j: (i, 0)),
            pl.BlockSpec((d, tf), lambda i, j: (0, j)),
            pl.BlockSpec((d, tf), lambda i, j: (0, nf + j)),
            pl.BlockSpec((tf, d), lambda i, j: (j, 0)),
            pl.BlockSpec((tm, d), lambda i, j: (i, 0)),
            pl.BlockSpec((1, d), lambda i, j: (0, 0)),
        ],
        out_specs=pl.BlockSpec((tm, d), lambda i, j: (i, 0)),
        out_shape=jax.ShapeDtypeStruct((t, d), F32),
        compiler_params=pltpu.CompilerParams(
            dimension_semantics=("parallel", "arbitrary"), vmem_limit_bytes=VMEM_LIMIT),
        name="ffn",
    )(h2, w_gate_up, w_gate_up, w_down, x1, norm_w)


def _trunk(x, s_hg, s_gla, wts, consts):
    b, t, d = x.shape
    xf = x.reshape(b * t, d)
    mall, lev = consts
    p, a_hg, a_gla = _inproj(xf, wts["norm_mix"], wts["w_in"], wts["w_gr"], wts["lb_logits"], wts["w_gk2"],
                             wts["b_gk"], TM_INPROJ)
    p = p.reshape(b, t, -1)
    o_hg, s_hg_new = _recurrence(p, a_hg.reshape(b, t, -1), s_hg, wts["hg_norm"], mall, lev, name="hgrn",
                                 q_col=0, k_col=IN_TILE, v_col=2 * IN_TILE, g_col=3 * IN_TILE)
    gla_q = 4 * IN_TILE
    o_gla, s_gla_new = _recurrence(p, a_gla.reshape(b, t, -1), s_gla, wts["gla_norm"], mall, lev, name="gla",
                                   q_col=gla_q, k_col=gla_q + GLA_HEADS * GLA_DK, v_col=5 * IN_TILE,
                                   g_col=6 * IN_TILE)
    x1, h2 = _outproj(o_hg.reshape(b * t, -1), o_gla.reshape(b * t, -1), wts["w_out_top"], wts["w_out_bot"],
                      xf, wts["norm_ffn"], TM_OUTPROJ)
    y = _ffn(h2, wts["w_gate_up"], wts["w_down"], x1, wts["norm_final"], TM_FFN, TF_FFN)
    return y.reshape(b, t, d), s_hg_new[None], s_gla_new[None]


def kernel(x_prompt, x_sample, state_hgrn, state_gla, lb_logits, norm_mix, w_in, w_gk2, b_gk, hg_norm, gla_norm,
           w_out, norm_ffn, w_gate_up, w_down, norm_final):
    w = w_in[0].astype(BF16)
    hw = HG_HEADS * HG_DK
    w_gr = jnp.pad(w[:, P_COLS:P_COLS + GLA_RANK], ((0, 0), (0, LANES - GLA_RANK)))
    w2 = jnp.pad(w_gk2[0], ((0, LANES - GLA_RANK), (0, 0))).astype(BF16)
    wo = w_out[0].astype(BF16)
    wts = {
        "norm_mix": norm_mix[0][None, :], "w_in": w, "w_gr": w_gr,
        "lb_logits": lb_logits, "hg_norm": hg_norm[0][None, :],
        "w_gk2": w2, "b_gk": b_gk[0][None, :], "gla_norm": gla_norm[0][None, :],
        "w_out_top": wo[:hw], "w_out_bot": wo[hw:], "norm_ffn": norm_ffn[0][None, :],
        "w_gate_up": w_gate_up[0].astype(BF16), "w_down": w_down[0].astype(BF16),
        "norm_final": norm_final[None, :],
    }
    mall_np, lev_np = _chunk_constants()
    consts = (jnp.asarray(mall_np, BF16), jnp.asarray(lev_np))

    bp = x_prompt.shape[0]
    zero_hg = jnp.zeros((bp, HG_HEADS, HG_DK, HG_DV), F32)
    zero_gla = jnp.zeros((bp, GLA_HEADS, GLA_DK, GLA_DV), F32)
    y_p, hg_p, gla_p = _trunk(x_prompt, zero_hg, zero_gla, wts, consts)
    y_s, hg_s, gla_s = _trunk(x_sample, state_hgrn[0], state_gla[0], wts, consts)
    return (y_p, y_s, hg_p, gla_p, hg_s, gla_s)
```

```python
import functools

import numpy as np
import jax
import jax.numpy as jnp
from jax import lax
from jax.experimental import pallas as pl
from jax.experimental.pallas import tpu as pltpu

F32 = jnp.float32
BF16 = jnp.bfloat16

HG_HEADS = 8
HG_DK = 128
HG_DV = 128
GLA_HEADS = 4
GLA_DK = 128
GLA_DV = 256
GLA_RANK = 16
GLA_GATE_NORMALIZER = 16.0
D_FF = 5632
EPS = 1e-6
LOG2E = 1.4426950408889634
LOG2_DECAY_FLOOR = -1e30

LANES = 128
CHUNK = 128
LEVELS = 7
MXU_LEVELS = 3
CHUNKS_PER_STEP = 32
VMEM_LIMIT = 60 * 1024 * 1024
TM_INPROJ = 1024
TM_OUTPROJ = 512
TM_FFN = 1024
TF_FFN = 512

IN_TILE = 1024
IN_TILES = 7
P_COLS = IN_TILE * IN_TILES


def _rms(x, w):
    ms = jnp.mean(x * x, axis=-1, keepdims=True)
    return (x * lax.rsqrt(ms + EPS)) * w


def _sigmoid(x):
    return 1.0 / (1.0 + jnp.exp(-x))


def _silu(x):
    return x * _sigmoid(x)


def _inproj_kernel(x_ref, nw_ref, w_ref, wr_ref, lbl_ref, w2_ref, b2_ref, p_ref, ahg_ref, agla_ref, h_scr):
    j = pl.program_id(1)
    hg = HG_HEADS * HG_DK
    gl = GLA_HEADS * GLA_DK

    @pl.when(j == 0)
    def _():
        h_scr[...] = _rms(x_ref[...], nw_ref[...]).astype(BF16)

    def forget_gate(z, cols):
        l0 = lbl_ref[0:1, cols]
        l1 = lbl_ref[1:2, cols]
        mx = jnp.maximum(l0, l1)
        e0 = jnp.exp(l0 - mx)
        e1 = jnp.exp(l1 - mx)
        lb = e0 / (e0 + e1)
        ez = jnp.exp(-jnp.abs(z))
        big = 1.0 / (1.0 + ez)
        small = ez * big
        pos = z >= 0.0
        a = jnp.maximum(jnp.log(lb + (1.0 - lb) * jnp.where(pos, big, small)) * LOG2E, LOG2_DECAY_FLOOR)
        hi = a.astype(BF16)
        ahg_ref[:, cols] = hi
        ahg_ref[:, hg + cols.start:hg + cols.stop] = (a - hi.astype(F32)).astype(BF16)
        return (1.0 - lb) * jnp.where(pos, small, big)

    whole = slice(0, IN_TILE)
    identity = lambda p, cols: p
    silu = lambda p, cols: _silu(p)
    pieces = (
        ((whole, lambda p, cols: _silu(p) * (HG_DK ** -0.5)),),
        ((whole, forget_gate),),
        ((whole, identity),),
        ((whole, silu),),
        ((slice(0, gl), lambda p, cols: p * (GLA_DK ** -0.5)), (slice(gl, IN_TILE), identity)),
        ((whole, identity),),
        ((whole, silu),),
    )

    def low_rank_gate():
        gr = jnp.dot(h_scr[...], wr_ref[...], preferred_element_type=F32).astype(BF16)
        x2 = (jnp.dot(gr, w2_ref[...], preferred_element_type=F32) + b2_ref[...]) * LOG2E
        a = (jnp.minimum(x2, 0.0) - jnp.log(1.0 + jnp.exp2(-jnp.abs(x2))) * LOG2E) * (1.0 / GLA_GATE_NORMALIZER)
        hi = a.astype(BF16)
        agla_ref[:, :gl] = hi
        agla_ref[:, gl:] = (a - hi.astype(F32)).astype(BF16)

    for tile in range(IN_TILES):
        @pl.when(j == tile)
        def _(tile=tile):
            for cols, activation in pieces[tile]:
                p = jnp.dot(h_scr[...], w_ref[:, cols], preferred_element_type=F32)
                p_ref[:, cols] = activation(p, cols).astype(BF16)
            if tile == IN_TILES - 1:
                low_rank_gate()


def _inproj(x, norm_w, w, w_gr, lb_logits, w2, b2, tm):
    t, d = x.shape
    hg = HG_HEADS * HG_DK
    gl = GLA_HEADS * GLA_DK
    const = lambda shape: pl.BlockSpec(shape, lambda i, j: (0,) * len(shape))
    return pl.pallas_call(
        _inproj_kernel,
        grid=(t // tm, IN_TILES),
        in_specs=[
            pl.BlockSpec((tm, d), lambda i, j: (i, 0)),
            const((1, d)),
            pl.BlockSpec((d, IN_TILE), lambda i, j: (0, j)),
            const((d, LANES)), const((2, hg)), const((LANES, gl)), const((1, gl)),
        ],
        out_specs=[pl.BlockSpec((tm, IN_TILE), lambda i, j: (i, j)),
                   pl.BlockSpec((tm, 2 * hg), lambda i, j: (i, 0)),
                   pl.BlockSpec((tm, 2 * gl), lambda i, j: (i, 0))],
        out_shape=[jax.ShapeDtypeStruct((t, P_COLS), BF16),
                   jax.ShapeDtypeStruct((t, 2 * hg), BF16),
                   jax.ShapeDtypeStruct((t, 2 * gl), BF16)],
        scratch_shapes=[pltpu.VMEM((tm, d), BF16)],
        compiler_params=pltpu.CompilerParams(
            dimension_semantics=("parallel", "arbitrary"), vmem_limit_bytes=VMEM_LIMIT),
        name="inproj",
    )(x, norm_w, w, w_gr, lb_logits, w2, b2)


def _chunk_constants():
    c = CHUNK
    t = np.arange(c)[:, None]
    u = np.arange(c)[None, :]
    mats = []
    for l in range(MXU_LEVELS):
        h = 1 << l
        mid = (t // (2 * h)) * (2 * h) + h - 1
        upper = (t & h) != 0
        mats.append(np.where(upper, (u > mid) & (u <= t), (u > t) & (u <= mid)))
    mats.append(u <= t)
    mall = np.concatenate(mats, axis=0).astype(np.float32)
    mall = np.concatenate([mall, mall], axis=1)
    x = t ^ u
    lev = np.where(t > u, np.floor(np.log2(np.maximum(x, 1))).astype(np.int32), -1)
    lev = np.where(t == u, LEVELS, lev).astype(np.int32)
    return mall, lev


def _aligned_operands(q, k, cum, l):
    h = 1 << l
    lows, ups = [], []
    for base in range(0, CHUNK, 2 * h):
        mid = cum[base + h - 1:base + h, :]
        lows.append(k[base:base + h] * jnp.exp2(mid - cum[base:base + h]))
        ups.append(q[base + h:base + 2 * h] * jnp.exp2(cum[base + h:base + 2 * h] - mid))
    x_up = jnp.concatenate(ups, axis=0).astype(BF16)
    x = jnp.concatenate([z for pair in zip(lows, ups) for z in pair], axis=0)
    return x_up, x.T.astype(BF16)


def _aligned_update(scores, sc, member, l):
    h = 1 << l
    parts = []
    for i, base in enumerate(range(0, CHUNK, 2 * h)):
        rows = slice(base + h, base + 2 * h)
        parts.append(scores[base:base + h])
        parts.append(jnp.where(member[rows], sc[i * h:(i + 1) * h], scores[rows]))
    return jnp.concatenate(parts, axis=0)


def _chunks_parallel(qs, ks, as_, vs, mall, lev):
    c = CHUNK
    n = len(qs)
    dv = vs[0].shape[1]
    e_all = jnp.dot(mall, jnp.concatenate(as_, axis=1), preferred_element_type=F32)
    es = [e_all[:, i * LANES:(i + 1) * LANES] for i in range(n)]
    cums = [e[MXU_LEVELS * c:] for e in es]

    row = lax.broadcasted_iota(jnp.int32, (c, LANES), 0)
    uppers = [(row & (1 << l)) != 0 for l in range(MXU_LEVELS)]
    tots = [cum[c - 1:c, :] for cum in cums]
    operands, qcs, krs = [], [], []
    for q, k, e, cum, tot in zip(qs, ks, es, cums, tots):
        ops = []
        for l in range(MXU_LEVELS):
            x = jnp.where(uppers[l], q, k) * jnp.exp2(e[l * c:(l + 1) * c])
            ops.append((x.astype(BF16), x.T.astype(BF16)))
        for l in range(MXU_LEVELS, LEVELS):
            ops.append(_aligned_operands(q, k, cum, l))
        operands.append(ops)
        qcs.append((q * jnp.exp2(cum)).astype(BF16))
        krs.append((k * jnp.exp2(tot - cum)).T.astype(BF16))
    products = [[jnp.dot(x_up, x_t, preferred_element_type=F32) for x_up, x_t in ops]
                for ops in operands]
    members = [lev == l for l in range(LEVELS)]
    scores = []
    for q, k, scs in zip(qs, ks, products):
        s = jnp.where(lev == LEVELS, jnp.sum(q * k, axis=1, keepdims=True), 0.0)
        for l in range(MXU_LEVELS):
            s = jnp.where(members[l], scs[l], s)
        for l in range(MXU_LEVELS, LEVELS):
            s = _aligned_update(s, scs[l], members[l], l)
        scores.append(s.astype(BF16))

    both = [jnp.dot(jnp.concatenate([s, kr_t], axis=0), v, preferred_element_type=F32)
            for s, kr_t, v in zip(scores, krs, vs)]
    ovs = [z[:c] for z in both]
    us = [z[c:] for z in both]
    dcols = [jnp.broadcast_to(jnp.exp2(tot), (LANES, LANES)).T for tot in tots]
    if dv > LANES:
        dcols = [jnp.concatenate([d] * (dv // LANES), axis=1) for d in dcols]
    return qcs, ovs, us, dcols


def _pad_rows(x, rows):
    if x.shape[0] == rows:
        return x
    return jnp.concatenate([x, jnp.zeros((rows - x.shape[0], x.shape[1]), x.dtype)], axis=0)


def _rec_kernel(q_ref, k_ref, v_ref, g_ref, ahi_ref, alo_ref, s0_ref, nw_ref, mall_ref, lev_ref,
                o_ref, sout_ref, s_scr, *, bb, tb):
    ti = pl.program_id(2)

    @pl.when(ti == 0)
    def _():
        s_scr[...] = s0_ref[...]

    rows = min(tb, CHUNK)
    items = [(b, slice(r0, r0 + rows)) for b in range(bb) for r0 in range(0, tb, rows)]
    qs = [_pad_rows(q_ref[b, sl, :].astype(F32), CHUNK) for b, sl in items]
    ks = [_pad_rows(k_ref[b, sl, :].astype(F32), CHUNK) for b, sl in items]
    vs = [_pad_rows(v_ref[b, sl, :], CHUNK) for b, sl in items]
    as_ = [jnp.concatenate([_pad_rows(ahi_ref[b, sl, :], CHUNK), _pad_rows(alo_ref[b, sl, :], CHUNK)], axis=0)
           for b, sl in items]
    qcs, ovs, us, dcols = _chunks_parallel(qs, ks, as_, vs, mall_ref[...], lev_ref[...])

    states = []
    for i, (b, sl) in enumerate(items):
        s = s_scr[b] if sl.start == 0 else s
        states.append(s)
        s = dcols[i] * s + us[i]
        if sl.stop == tb:
            s_scr[b] = s
    outs = [jnp.dot(qc, s.astype(BF16), preferred_element_type=F32) + ov for qc, s, ov in zip(qcs, states, ovs)]

    nw = nw_ref[...]
    for (b, sl), o in zip(items, outs):
        o = o[:rows]
        ms = jnp.mean(o * o, axis=-1, keepdims=True)
        y = o * lax.rsqrt(ms + EPS) * nw
        o_ref[b, sl, :] = (y * g_ref[b, sl, :].astype(F32)).astype(BF16)

    @pl.when(ti == pl.num_programs(2) - 1)
    def _():
        sout_ref[...] = s_scr[...]


def _rec_blocks(t_len, batch):
    tb = min(t_len, CHUNKS_PER_STEP * CHUNK)
    bb = 1 if t_len > CHUNK else min(batch, CHUNKS_PER_STEP)
    return bb, tb


def _recurrence(p, a, s0, norm_w, mall, lev, *, name, q_col, k_col, v_col, g_col):
    b, t, _ = p.shape
    _, heads, dk, dv = s0.shape
    bb, tb = _rec_blocks(t, b)
    blk = lambda width, col: pl.BlockSpec((bb, tb, width), lambda bi, h, ti: (bi, ti, col // width + h))
    const = lambda shape: pl.BlockSpec(shape, lambda bi, h, ti: (0,) * len(shape))
    st = pl.BlockSpec((bb, None, dk, dv), lambda bi, h, ti: (bi, h, 0, 0))
    return pl.pallas_call(
        functools.partial(_rec_kernel, bb=bb, tb=tb),
        grid=(b // bb, heads, t // tb),
        in_specs=[blk(dk, q_col), blk(dk, k_col), blk(dv, v_col), blk(dv, g_col),
                  blk(dk, 0), blk(dk, heads * dk), st, const((1, dv)), const(mall.shape), const(lev.shape)],
        out_specs=[blk(dv, 0), st],
        out_shape=[jax.ShapeDtypeStruct((b, t, heads * dv), BF16), jax.ShapeDtypeStruct(s0.shape, F32)],
        scratch_shapes=[pltpu.VMEM((bb, dk, dv), F32)],
        compiler_params=pltpu.CompilerParams(
            dimension_semantics=("parallel", "parallel", "arbitrary"), vmem_limit_bytes=VMEM_LIMIT),
        name=name,
    )(p, p, p, p, a, a, s0, norm_w, mall, lev)


def _outproj_kernel(ohg_ref, ogla_ref, wt_ref, wb_ref, x_ref, nw_ref, x1_ref, h2_ref):
    m = (jnp.dot(ohg_ref[...], wt_ref[...], preferred_element_type=F32)
         + jnp.dot(ogla_ref[...], wb_ref[...], preferred_element_type=F32))
    x1 = x_ref[...] + m
    x1_ref[...] = x1
    h2_ref[...] = _rms(x1, nw_ref[...]).astype(BF16)


def _outproj(o_hg, o_gla, w_top, w_bot, x, norm_w, tm):
    t, d = x.shape
    half = o_hg.shape[1]
    return pl.pallas_call(
        _outproj_kernel,
        grid=(t // tm,),
        in_specs=[
            pl.BlockSpec((tm, half), lambda i: (i, 0)),
            pl.BlockSpec((tm, half), lambda i: (i, 0)),
            pl.BlockSpec((half, d), lambda i: (0, 0)),
            pl.BlockSpec((half, d), lambda i: (0, 0)),
            pl.BlockSpec((tm, d), lambda i: (i, 0)),
            pl.BlockSpec((1, d), lambda i: (0, 0)),
        ],
        out_specs=[pl.BlockSpec((tm, d), lambda i: (i, 0)), pl.BlockSpec((tm, d), lambda i: (i, 0))],
        out_shape=[jax.ShapeDtypeStruct((t, d), F32), jax.ShapeDtypeStruct((t, d), BF16)],
        compiler_params=pltpu.CompilerParams(
            dimension_semantics=("parallel",), vmem_limit_bytes=VMEM_LIMIT),
        name="outproj",
    )(o_hg, o_gla, w_top, w_bot, x, norm_w)


def _ffn_kernel(h_ref, wg_ref, wu_ref, wd_ref, x1_ref, nw_ref, o_ref):
    j = pl.program_id(1)

    def down():
        h = h_ref[...]
        g = jnp.dot(h, wg_ref[...], preferred_element_type=F32)
        u = jnp.dot(h, wu_ref[...], preferred_element_type=F32)
        mid = (_silu(g) * u).astype(BF16)
        return jnp.dot(mid, wd_ref[...], preferred_element_type=F32)

    @pl.when(j == 0)
    def _():
        o_ref[...] = x1_ref[...] + down()

    @pl.when(j > 0)
    def _():
        o_ref[...] += down()

    @pl.when(j == pl.num_programs(1) - 1)
    def _():
        o_ref[...] = _rms(o_ref[...], nw_ref[...])


def _ffn(h2, w_gate_up, w_down, x1, norm_w, tm, tf):
    t, d = x1.shape
    nf = D_FF // tf
    return pl.pallas_call(
        _ffn_kernel,
        grid=(t // tm, nf),
        in_specs=[
            pl.BlockSpec((tm, d), lambda i, j: (i, 0)),
            pl.BlockSpec((d, tf), lambda i, j: (0, j)),
            pl.BlockSpec((d, tf), lambda i, j: (0, nf + j)),
            pl.BlockSpec((tf, d), lambda i, j: (j, 0)),
            pl.BlockSpec((tm, d), lambda i, j: (i, 0)),
            pl.BlockSpec((1, d), lambda i, j: (0, 0)),
        ],
        out_specs=pl.BlockSpec((tm, d), lambda i, j: (i, 0)),
        out_shape=jax.ShapeDtypeStruct((t, d), F32),
        compiler_params=pltpu.CompilerParams(
            dimension_semantics=("parallel", "arbitrary"), vmem_limit_bytes=VMEM_LIMIT),
        name="ffn",
    )(h2, w_gate_up, w_gate_up, w_down, x1, norm_w)


def _trunk(x, s_hg, s_gla, wts, consts):
    b, t, d = x.shape
    xf = x.reshape(b * t, d)
    mall, lev = consts
    p, a_hg, a_gla = _inproj(xf, wts["norm_mix"], wts["w_in"], wts["w_gr"], wts["lb_logits"], wts["w_gk2"],
                             wts["b_gk"], TM_INPROJ)
    p = p.reshape(b, t, -1)
    o_hg, s_hg_new = _recurrence(p, a_hg.reshape(b, t, -1), s_hg, wts["hg_norm"], mall, lev, name="hgrn",
                                 q_col=0, k_col=IN_TILE, v_col=2 * IN_TILE, g_col=3 * IN_TILE)
    gla_q = 4 * IN_TILE
    o_gla, s_gla_new = _recurrence(p, a_gla.reshape(b, t, -1), s_gla, wts["gla_norm"], mall, lev, name="gla",
                                   q_col=gla_q, k_col=gla_q + GLA_HEADS * GLA_DK, v_col=5 * IN_TILE,
                                   g_col=6 * IN_TILE)
    x1, h2 = _outproj(o_hg.reshape(b * t, -1), o_gla.reshape(b * t, -1), wts["w_out_top"], wts["w_out_bot"],
                      xf, wts["norm_ffn"], TM_OUTPROJ)
    y = _ffn(h2, wts["w_gate_up"], wts["w_down"], x1, wts["norm_final"], TM_FFN, TF_FFN)
    return y.reshape(b, t, d), s_hg_new[None], s_gla_new[None]


def kernel(x_prompt, x_sample, state_hgrn, state_gla, lb_logits, norm_mix, w_in, w_gk2, b_gk, hg_norm, gla_norm,
           w_out, norm_ffn, w_gate_up, w_down, norm_final):
    w = w_in[0].astype(BF16)
    hw = HG_HEADS * HG_DK
    w_gr = jnp.pad(w[:, P_COLS:P_COLS + GLA_RANK], ((0, 0), (0, LANES - GLA_RANK)))
    w2 = jnp.pad(w_gk2[0], ((0, LANES - GLA_RANK), (0, 0))).astype(BF16)
    wo = w_out[0].astype(BF16)
    wts = {
        "norm_mix": norm_mix[0][None, :], "w_in": w, "w_gr": w_gr,
        "lb_logits": lb_logits, "hg_norm": hg_norm[0][None, :],
        "w_gk2": w2, "b_gk": b_gk[0][None, :], "gla_norm": gla_norm[0][None, :],
        "w_out_top": wo[:hw], "w_out_bot": wo[hw:], "norm_ffn": norm_ffn[0][None, :],
        "w_gate_up": w_gate_up[0].astype(BF16), "w_down": w_down[0].astype(BF16),
        "norm_final": norm_final[None, :],
    }
    mall_np, lev_np = _chunk_constants()
    consts = (jnp.asarray(mall_np, BF16), jnp.asarray(lev_np))

    bp = x_prompt.shape[0]
    zero_hg = jnp.zeros((bp, HG_HEADS, HG_DK, HG_DV), F32)
    zero_gla = jnp.zeros((bp, GLA_HEADS, GLA_DK, GLA_DV), F32)
    y_p, hg_p, gla_p = _trunk(x_prompt, zero_hg, zero_gla, wts, consts)
    y_s, hg_s, gla_s = _trunk(x_sample, state_hgrn[0], state_gla[0], wts, consts)
    return (y_p, y_s, hg_p, gla_p, hg_s, gla_s)
```
